```python
import math
import jax, jax.numpy as jnp
from jax import lax
import numpy as np

D_MODEL = 2048
BATCH = 4
SEQ = 4096
DEPTH = 2
DEC_BATCH = 8
DEC_SEQ = 32
PAST_LEN = 1024

CHUNK = 64
N_A_LAYERS = DEPTH // 2
N_B_LAYERS = DEPTH - N_A_LAYERS
GDN_K_HEADS = 16
GDN_V_HEADS = 32
GDN_DK = 128
GDN_DV = 128
GDN_KEY_DIM = GDN_K_HEADS * GDN_DK
GDN_VAL_DIM = GDN_V_HEADS * GDN_DV
CONV_W = 4
CONV_DIM = 2 * GDN_KEY_DIM + GDN_VAL_DIM
GDN_IN_DIM = CONV_DIM + GDN_VAL_DIM + 2 * GDN_V_HEADS
SB_Q_HEADS = 16
SB_KV_HEADS = 4
SB_HEAD_DIM = 128
SB_QBLK = 128
N_EXPERTS = 64
TOP_K = 8
N_GROUPS = 8
TOPK_GROUPS = 4
D_EXPERT = 512
D_SHARED = 512
ROUTED_SCALE = 2.5
MOE_BLK = 128
NORM_EPS = 1e-6

kernel_name = 'yoco_gdn_stickbreaking_moe_adaln_step'


def rms_norm(x):
    xf = x.astype(jnp.float32)
    return (xf * lax.rsqrt(jnp.mean(xf * xf, axis=-1, keepdims=True) + NORM_EPS)).astype(x.dtype)


def modulate(x, shift, scale):
    return x * (1 + scale[:, None, :]) + shift[:, None, :]


def l2_norm(x):
    return x * lax.rsqrt(jnp.sum(x * x, axis=-1, keepdims=True) + NORM_EPS)


def gated_delta_rule(q, k, v, g, beta, s0):
    b, t, h, dk = q.shape
    dv = v.shape[-1]
    n = -(-t // CHUNK)
    pad = n * CHUNK - t

    def blocks(a):
        a = jnp.pad(a, [(0, 0), (0, pad)] + [(0, 0)] * (a.ndim - 2))
        a = a.reshape((b, n, CHUNK) + a.shape[2:])
        return jnp.moveaxis(a, 3, 1)

    q, k, v, g, beta = (blocks(a) for a in (q * dk ** -0.5, k, v, g, beta))
    gc = jnp.cumsum(g, axis=-1)
    idx = jnp.arange(CHUNK)
    incl = idx[:, None] >= idx[None, :]
    strict = idx[:, None] > idx[None, :]
    decay = jnp.exp(jnp.where(incl, gc[..., :, None] - gc[..., None, :], -jnp.inf))
    kb = k * beta[..., None]
    kk = jnp.einsum('bhncd,bhnsd->bhncs', kb, k)
    lmat = jnp.where(strict, kk * decay, 0.0) + jnp.eye(CHUNK, dtype=kk.dtype)
    rhs = jnp.concatenate([v * beta[..., None], kb * jnp.exp(gc)[..., None]], axis=-1)
    sol = lax.linalg.triangular_solve(lmat, rhs, left_side=True, lower=True, unit_diagonal=True)
    u, w = sol[..., :dv], sol[..., dv:]
    attn = jnp.einsum('bhncd,bhnsd->bhncs', q, k) * decay
    q_dec = q * jnp.exp(gc)[..., None]
    g_last = gc[..., -1]
    k_tail = k * jnp.exp(g_last[..., None] - gc)[..., None]

    def step(s, xs):
        u_i, w_i, attn_i, qd_i, kt_i, gl_i = xs
        v_new = u_i - jnp.einsum('bhcd,bhde->bhce', w_i, s)
        o_i = jnp.einsum('bhcd,bhde->bhce', qd_i, s) + jnp.einsum('bhcs,bhse->bhce', attn_i, v_new)
        s = s * jnp.exp(gl_i)[..., None, None] + jnp.einsum('bhcd,bhce->bhde', kt_i, v_new)
        return s, o_i

    xs = tuple(jnp.moveaxis(a, 2, 0) for a in (u, w, attn, q_dec, k_tail, g_last))
    s_fin, o = lax.scan(step, s0, xs)
    o = jnp.moveaxis(jnp.moveaxis(o, 0, 2), 1, 3).reshape(b, n * CHUNK, h, dv)[:, :t]
    return o, s_fin


def gated_deltanet(hn, s0, conv0, w_in, conv_w, a_log, dt_bias, o_norm, w_out):
    b, t, _ = hn.shape
    proj = hn @ w_in
    qkv, z, beta_in, a_in = jnp.split(proj, [CONV_DIM, CONV_DIM + GDN_VAL_DIM, CONV_DIM + GDN_VAL_DIM + GDN_V_HEADS], axis=-1)
    full = jnp.concatenate([conv0.astype(qkv.dtype), qkv], axis=1)
    new_conv = full[:, -(CONV_W - 1):]
    conv = full[:, 0:t] * conv_w[0]
    for i in range(1, CONV_W):
        conv = conv + full[:, i:i + t] * conv_w[i]
    conv = jax.nn.silu(conv.astype(jnp.float32))
    q, k, v = jnp.split(conv, [GDN_KEY_DIM, 2 * GDN_KEY_DIM], axis=-1)
    rep = GDN_V_HEADS // GDN_K_HEADS
    q = jnp.repeat(l2_norm(q.reshape(b, t, GDN_K_HEADS, GDN_DK)), rep, axis=2)
    k = jnp.repeat(l2_norm(k.reshape(b, t, GDN_K_HEADS, GDN_DK)), rep, axis=2)
    v = v.reshape(b, t, GDN_V_HEADS, GDN_DV)
    beta = jax.nn.sigmoid(beta_in.astype(jnp.float32))
    g = -jnp.exp(a_log.astype(jnp.float32)) * jax.nn.softplus(a_in.astype(jnp.float32) + dt_bias.astype(jnp.float32))
    o, s_fin = gated_delta_rule(q, k, v, g, beta, s0.astype(jnp.float32))
    zf = z.astype(jnp.float32).reshape(b, t, GDN_V_HEADS, GDN_DV)
    o = rms_norm(o) * o_norm.astype(jnp.float32) * jax.nn.silu(zf)
    out = o.reshape(b, t, GDN_VAL_DIM).astype(hn.dtype) @ w_out
    return out, s_fin.astype(s0.dtype), new_conv.astype(conv0.dtype)


def stick_breaking(q, k, v, q_start):
    b, tq = q.shape[:2]
    grp = SB_Q_HEADS // SB_KV_HEADS
    outs = []
    for i0 in range(0, tq, SB_QBLK):
        qb = q[:, i0:i0 + SB_QBLK]
        nq = qb.shape[1]
        kend = min(k.shape[1], q_start + i0 + nq)
        kb, vb = k[:, :kend], v[:, :kend]
        qpos = q_start + i0 + jnp.arange(nq)
        kpos = jnp.arange(kend)
        causal = kpos[None, :] < qpos[:, None]
        z = jnp.einsum('bqhgd,bkhd->bhgqk', qb.reshape(b, nq, SB_KV_HEADS, grp, SB_HEAD_DIM), kb).astype(jnp.float32) * SB_HEAD_DIM ** -0.5
        log_1m = jnp.where(causal, jax.nn.log_sigmoid(-z), 0.0)
        after = lax.cumsum(log_1m, axis=4, reverse=True) - log_1m
        a = jnp.where(causal, jnp.exp(jax.nn.log_sigmoid(z) + after), 0.0)
        o = jnp.einsum('bhgqk,bkhd->bqhgd', a.astype(vb.dtype), vb)
        outs.append(o.reshape(b, nq, SB_Q_HEADS * SB_HEAD_DIM))
    return jnp.concatenate(outs, axis=1)


def moe_ffn(x, router_w, router_bias, w_gate_up, w_down, ws_gate_up, ws_down):
    n, d = x.shape
    scores = jax.nn.sigmoid((x @ router_w).astype(jnp.float32))
    biased = scores + router_bias.astype(jnp.float32)
    grp_score = lax.top_k(biased.reshape(n, N_GROUPS, N_EXPERTS // N_GROUPS), 2)[0].sum(-1)
    _, top_g = lax.top_k(grp_score, TOPK_GROUPS)
    gmask = jnp.any(top_g[:, :, None] == jnp.arange(N_GROUPS)[None, None, :], axis=1)
    emask = jnp.repeat(gmask, N_EXPERTS // N_GROUPS, axis=1)
    _, idx = lax.top_k(jnp.where(emask, biased, -jnp.inf), TOP_K)
    wts = jnp.take_along_axis(scores, idx, axis=-1)
    wts = wts / jnp.sum(wts, axis=-1, keepdims=True) * ROUTED_SCALE
    m = n * TOP_K
    flat_e = idx.reshape(m)
    order = jnp.argsort(flat_e)
    sorted_e = flat_e[order]
    tok = (order // TOP_K).astype(jnp.int32)
    w_sorted = wts.reshape(m)[order]
    counts = jnp.bincount(flat_e, length=N_EXPERTS)
    padded = (counts + MOE_BLK - 1) // MOE_BLK * MOE_BLK
    pad_end = jnp.cumsum(padded)
    pad_start = pad_end - padded
    start = jnp.cumsum(counts) - counts
    dest = (pad_start[sorted_e] + jnp.arange(m) - start[sorted_e]).astype(jnp.int32)
    nb = -(-(m + N_EXPERTS * (MOE_BLK - 1)) // MOE_BLK)
    p = nb * MOE_BLK
    src = jnp.full((p,), n, jnp.int32).at[dest].set(tok)
    wd = jnp.zeros((p,), jnp.float32).at[dest].set(w_sorted)
    blk_e = jnp.minimum(jnp.searchsorted(pad_end, jnp.arange(nb) * MOE_BLK, side='right'), N_EXPERTS - 1)
    x_pad = jnp.concatenate([x, jnp.zeros((1, d), x.dtype)], axis=0)

    def step(acc, xs):
        s, wb, e = xs
        gu = x_pad[s] @ w_gate_up[e]
        gt, up = jnp.split(gu, 2, axis=-1)
        y = (jax.nn.silu(gt) * up) @ w_down[e]
        return acc.at[s].add(y * wb[:, None].astype(y.dtype)), None

    acc, _ = lax.scan(step, jnp.zeros((n + 1, d), x.dtype), (src.reshape(nb, MOE_BLK), wd.reshape(nb, MOE_BLK), blk_e))
    sg, su = jnp.split(x @ ws_gate_up, 2, axis=-1)
    return acc[:n] + (jax.nn.silu(sg) * su) @ ws_down


def trunk(x, c, s_delta, s_conv, past_k, past_v, ada_w, ada_b, a_w_in, a_conv_w, a_a_log, a_dt_bias, a_o_norm, a_w_out,
          kv_ada_w, kv_ada_b, w_kv, b_w_q, b_w_out, router_w, router_bias, w_gate_up, w_down, ws_gate_up, ws_down, final_norm):
    b, t, d = x.shape
    past = past_k.shape[1]
    c_act = jax.nn.silu(c)
    new_s, new_conv = [], []
    k_new = v_new = k_all = v_all = None
    for l in range(DEPTH):
        mod = c_act @ ada_w[l] + ada_b[l]
        sh_m, sc_m, gt_m, sh_f, sc_f, gt_f = jnp.split(mod, 6, axis=-1)
        if l < N_A_LAYERS:
            hn = modulate(rms_norm(x), sh_m, sc_m)
            out, s_l, conv_l = gated_deltanet(hn, s_delta[l], s_conv[l], a_w_in[l], a_conv_w[l], a_a_log[l],
                                              a_dt_bias[l], a_o_norm[l], a_w_out[l])
            new_s.append(s_l)
            new_conv.append(conv_l)
        else:
            if l == N_A_LAYERS:
                kv_sh, kv_sc = jnp.split(c_act @ kv_ada_w + kv_ada_b, 2, axis=-1)
                kvh = (modulate(rms_norm(x), kv_sh, kv_sc) @ w_kv).reshape(b, t, 2 * SB_KV_HEADS, SB_HEAD_DIM)
                k_new, v_new = jnp.split(kvh, 2, axis=2)
                k_all = jnp.concatenate([past_k.astype(x.dtype), k_new], axis=1)
                v_all = jnp.concatenate([past_v.astype(x.dtype), v_new], axis=1)
            j = l - N_A_LAYERS
            hn = modulate(rms_norm(x), sh_m, sc_m)
            q = (hn @ b_w_q[j]).reshape(b, t, SB_Q_HEADS, SB_HEAD_DIM)
            out = stick_breaking(q, k_all, v_all, past) @ b_w_out[j]
        x = x + gt_m[:, None, :] * out
        hn = modulate(rms_norm(x), sh_f, sc_f)
        ff = moe_ffn(hn.reshape(b * t, d), router_w[l], router_bias[l], w_gate_up[l], w_down[l], ws_gate_up[l], ws_down[l])
        x = x + gt_f[:, None, :] * ff.reshape(b, t, d)
    y = rms_norm(x) * final_norm
    return y, k_new, v_new, jnp.stack(new_s), jnp.stack(new_conv)


def setup_inputs(seed: int = 0) -> dict:
    key = jax.random.key(seed)
    ks = jax.random.split(key, 32)
    f32 = jnp.float32
    D = D_MODEL

    def nrm(k, shape, scale):
        return jax.random.normal(k, shape, f32) * scale

    dt = jnp.exp(jax.random.uniform(ks[13], (N_A_LAYERS, GDN_V_HEADS), f32, math.log(1e-3), math.log(1e-1)))
    return {
        'x_prompt': nrm(ks[0], (BATCH, SEQ, D), 1.0),
        'x_sample': nrm(ks[1], (DEC_BATCH, DEC_SEQ, D), 1.0),
        'c_prompt': nrm(ks[2], (BATCH, D), 1.0),
        'c_sample': nrm(ks[3], (DEC_BATCH, D), 1.0),
        'state_delta': nrm(ks[4], (N_A_LAYERS, DEC_BATCH, GDN_V_HEADS, GDN_DK, GDN_DV), GDN_DK ** -0.5),
        'state_conv': nrm(ks[5], (N_A_LAYERS, DEC_BATCH, CONV_W - 1, CONV_DIM), 1.0),
        'cache_k': nrm(ks[6], (DEC_BATCH, PAST_LEN, SB_KV_HEADS, SB_HEAD_DIM), 1.0),
        'cache_v': nrm(ks[7], (DEC_BATCH, PAST_LEN, SB_KV_HEADS, SB_HEAD_DIM), 1.0),
        'ada_w': nrm(ks[8], (DEPTH, D, 6 * D), 0.5 * D ** -0.5),
        'ada_b': nrm(ks[9], (DEPTH, 6 * D), 0.02),
        'a_w_in': nrm(ks[10], (N_A_LAYERS, D, GDN_IN_DIM), D ** -0.5),
        'a_conv_w': nrm(ks[11], (N_A_LAYERS, CONV_W, CONV_DIM), CONV_W ** -0.5),
        'a_a_log': jnp.log(jax.random.uniform(ks[12], (N_A_LAYERS, GDN_V_HEADS), f32, 1.0, 16.0)),
        'a_dt_bias': dt + jnp.log(-jnp.expm1(-dt)),
        'a_o_norm': 1.0 + nrm(ks[14], (N_A_LAYERS, GDN_DV), 0.02),
        'a_w_out': nrm(ks[15], (N_A_LAYERS, GDN_VAL_DIM, D), GDN_VAL_DIM ** -0.5),
        'kv_ada_w': nrm(ks[16], (D, 2 * D), 0.5 * D ** -0.5),
        'kv_ada_b': nrm(ks[17], (2 * D,), 0.02),
        'w_kv': nrm(ks[18], (D, 2 * SB_KV_HEADS * SB_HEAD_DIM), D ** -0.5),
        'b_w_q': nrm(ks[19], (N_B_LAYERS, D, SB_Q_HEADS * SB_HEAD_DIM), D ** -0.5),
        'b_w_out': nrm(ks[20], (N_B_LAYERS, SB_Q_HEADS * SB_HEAD_DIM, D), (SB_Q_HEADS * SB_HEAD_DIM) ** -0.5),
        'router_w': nrm(ks[21], (DEPTH, D, N_EXPERTS), D ** -0.5),
        'router_bias': nrm(ks[22], (DEPTH, N_EXPERTS), 0.01),
        'w_gate_up': nrm(ks[23], (DEPTH, N_EXPERTS, D, 2 * D_EXPERT), D ** -0.5),
        'w_down': nrm(ks[24], (DEPTH, N_EXPERTS, D_EXPERT, D), D_EXPERT ** -0.5),
        'ws_gate_up': nrm(ks[25], (DEPTH, D, 2 * D_SHARED), D ** -0.5),
        'ws_down': nrm(ks[26], (DEPTH, D_SHARED, D), D_SHARED ** -0.5),
        'final_norm': 1.0 + nrm(ks[27], (D,), 0.02),
    }


def reference(x_prompt, x_sample, c_prompt, c_sample, state_delta, state_conv, cache_k, cache_v, ada_w, ada_b,
              a_w_in, a_conv_w, a_a_log, a_dt_bias, a_o_norm, a_w_out, kv_ada_w, kv_ada_b, w_kv, b_w_q, b_w_out,
              router_w, router_bias, w_gate_up, w_down, ws_gate_up, ws_down, final_norm):
    weights = (ada_w, ada_b, a_w_in, a_conv_w, a_a_log, a_dt_bias, a_o_norm, a_w_out, kv_ada_w, kv_ada_b, w_kv,
               b_w_q, b_w_out, router_w, router_bias, w_gate_up, w_down, ws_gate_up, ws_down, final_norm)
    bp = x_prompt.shape[0]
    zero_s = jnp.zeros((N_A_LAYERS, bp) + state_delta.shape[2:], state_delta.dtype)
    zero_conv = jnp.zeros((N_A_LAYERS, bp) + state_conv.shape[2:], state_conv.dtype)
    empty_kv = jnp.zeros((bp, 0) + cache_k.shape[2:], cache_k.dtype)
    y_prompt, k_prompt, v_prompt, delta_prompt, conv_prompt = trunk(
        x_prompt, c_prompt, zero_s, zero_conv, empty_kv, empty_kv, *weights)
    y_sample, k_sample, v_sample, delta_sample, conv_sample = trunk(
        x_sample, c_sample, state_delta, state_conv, cache_k, cache_v, *weights)
    return (y_prompt, y_sample, k_prompt, v_prompt, k_sample, v_sample, delta_prompt, conv_prompt, delta_sample, conv_sample)
```

```python
import functools

import jax
import jax.numpy as jnp
from jax import lax
from jax.experimental import pallas as pl
from jax.experimental.pallas import tpu as pltpu

F32, BF16, I32, U32 = jnp.float32, jnp.bfloat16, jnp.int32, jnp.uint32

NORM_EPS = 1e-6
CHUNK = 64
CONV_W = 4
N_EXPERTS = 64
TOP_K = 8
N_GROUPS = 8
TOPK_GROUPS = 4
ROUTED_SCALE = 2.5
HEAD_DIM = 128
GROUP_ROWS = 32
MOE_BM = 256
VMEM_LIMIT_BYTES = 56 * 1024 * 1024


def _cparams(*sem):
    return pltpu.CompilerParams(dimension_semantics=sem, vmem_limit_bytes=VMEM_LIMIT_BYTES)


def _sigmoid(x):
    return 1.0 / (1.0 + jnp.exp(-x))


def _silu(x):
    return x * _sigmoid(x)


def _softplus(x):
    return jnp.maximum(x, 0.0) + jnp.log(1.0 + jnp.exp(-jnp.abs(x)))


def _rms(x):
    return x * lax.rsqrt(jnp.mean(x * x, axis=-1, keepdims=True) + NORM_EPS)


def _pack_pair(lo, hi):
    lo_bits = lax.bitcast_convert_type(lo.astype(BF16).astype(F32), U32) >> 16
    hi_bits = lax.bitcast_convert_type(hi.astype(BF16).astype(F32), U32) & jnp.uint32(0xFFFF0000)
    return lo_bits | hi_bits


def _unpack_pair(w):
    lo = lax.bitcast_convert_type(w << 16, F32)
    hi = lax.bitcast_convert_type(w & jnp.uint32(0xFFFF0000), F32)
    return lo, hi


def _linear_kernel(*refs, norm, silu_in, has_mod, has_bias, has_res, prologue, n_sub):
    it = iter(refs)
    x_ref, w_ref = next(it), next(it)
    sh_ref = sc_ref = b_ref = res_ref = gate_ref = xs_ref = None
    if has_mod:
        sh_ref, sc_ref = next(it), next(it)
    if has_bias:
        b_ref = next(it)
    if has_res:
        res_ref, gate_ref = next(it), next(it)
    o_ref = next(it)
    if prologue:
        xs_ref = next(it)

        @pl.when(pl.program_id(1) == 0)
        def _():
            def prep(x):
                x = x.astype(F32)
                if silu_in:
                    x = _silu(x)
                if norm:
                    x = _rms(x)
                return x

            if has_mod:
                for g in range(n_sub):
                    rows = pl.ds(g * GROUP_ROWS, GROUP_ROWS)
                    xg = prep(x_ref[rows, :])
                    xg = xg * (1.0 + sc_ref[g:g + 1, :]) + sh_ref[g:g + 1, :]
                    xs_ref[rows, :] = xg.astype(BF16)
            else:
                xs_ref[...] = prep(x_ref[...]).astype(BF16)

        xs = xs_ref[...]
    else:
        xs = x_ref[...]
    acc = jnp.dot(xs, w_ref[...].astype(BF16), preferred_element_type=F32)
    if has_bias:
        acc = acc + b_ref[...]
    if has_res:
        for g in range(n_sub):
            rows = pl.ds(g * GROUP_ROWS, GROUP_ROWS)
            part = acc[g * GROUP_ROWS:(g + 1) * GROUP_ROWS, :] * gate_ref[g:g + 1, :]
            o_ref[rows, :] = (res_ref[rows, :] + part).astype(o_ref.dtype)
    else:
        o_ref[...] = acc.astype(o_ref.dtype)


def _linear(x, w, *, tm, tn, out_dtype, norm=False, silu_in=False, mod=None, bias=None, res=None, gate=None,
            name="linear"):
    m, k = x.shape
    n = w.shape[1]
    assert m % tm == 0 and n % tn == 0, (m, tm, n, tn)
    has_mod, has_bias, has_res = mod is not None, bias is not None, res is not None
    prologue = norm or silu_in or has_mod or x.dtype != BF16
    n_sub = tm // GROUP_ROWS if (has_mod or has_res) else 0
    in_specs = [pl.BlockSpec((tm, k), lambda i, j: (i, 0)), pl.BlockSpec((k, tn), lambda i, j: (0, j))]
    args = [x, w]
    if has_mod:
        in_specs += [pl.BlockSpec((n_sub, k), lambda i, j: (i, 0))] * 2
        args += list(mod)
    if has_bias:
        in_specs.append(pl.BlockSpec((1, tn), lambda i, j: (0, j)))
        args.append(bias.reshape(1, n))
    if has_res:
        in_specs += [pl.BlockSpec((tm, tn), lambda i, j: (i, j)), pl.BlockSpec((n_sub, tn), lambda i, j: (i, j))]
        args += [res, gate]
    return pl.pallas_call(
        functools.partial(_linear_kernel, norm=norm, silu_in=silu_in, has_mod=has_mod, has_bias=has_bias,
                          has_res=has_res, prologue=prologue, n_sub=n_sub),
        grid=(m // tm, n // tn),
        in_specs=in_specs,
        out_specs=pl.BlockSpec((tm, tn), lambda i, j: (i, j)),
        out_shape=jax.ShapeDtypeStruct((m, n), out_dtype),
        scratch_shapes=[pltpu.VMEM((tm, k), BF16)] if prologue else [],
        compiler_params=_cparams("parallel", "arbitrary"),
        name=name,
    )(*args)


def _split_bf16(a):
    hi = a.astype(BF16)
    lo = (a - hi.astype(F32)).astype(BF16)
    return hi, lo


def _bmm(a, b):
    return jnp.einsum("bij,bjk->bik", a, b, preferred_element_type=F32)


def _bmm3(a, b):
    ah, al = _split_bf16(a)
    bh, bl = _split_bf16(b)
    return _bmm(ah, bh) + _bmm(al, bh) + _bmm(ah, bl)


def _unit_lower_inverse(a, c):
    r = lax.broadcasted_iota(I32, (c, c), 0)
    q = lax.broadcasted_iota(I32, (c, c), 1)
    same = lambda s: (r ^ q) < s
    eye = (r == q).astype(F32)
    a0 = jnp.where(same(8), a, 0.0)
    a2 = _bmm3(a0, a0)
    a4 = _bmm3(a2, a2)
    t = eye - a0
    t = t + _bmm3(t, a2)
    t = t + _bmm3(t, a4)
    s = 8
    while s < c:
        a_off = jnp.where(same(2 * s) & jnp.logical_not(same(s)), a, 0.0)
        t = t - _bmm3(_bmm3(t, a_off), t)
        s *= 2
    return t


def _gdn_kernel(q_ref, k_ref, v_ref, z_ref, ba_ref, al_ref, dtb_ref, cwq_ref, cwk_ref, cwv_ref,
                c0q_ref, c0k_ref, c0v_ref, s0_ref, on_ref, o_ref, sfin_ref,
                xq_s, xk_s, xv_s, state_s, *, tt_rows, chunk):
    tt = pl.program_id(2)
    n_tt = pl.num_programs(2)
    d = HEAD_DIM
    nc = tt_rows // chunk
    halo = CONV_W - 1

    @pl.when(tt == 0)
    def _():
        state_s[...] = s0_ref[0]
        xq_s[8 - halo:8, :] = c0q_ref[0]
        xk_s[8 - halo:8, :] = c0k_ref[0]
        xv_s[8 - halo:8, :] = c0v_ref[0]

    def conv_silu(x_ref, xs, cw_ref):
        xs[8:8 + tt_rows, :] = x_ref[...]
        acc = xs[8 - halo:8 - halo + tt_rows, :] * cw_ref[0:1, :]
        for i in range(1, CONV_W):
            acc = acc + xs[8 - halo + i:8 - halo + i + tt_rows, :] * cw_ref[i:i + 1, :]
        xs[8 - halo:8, :] = xs[8 + tt_rows - halo:8 + tt_rows, :]
        return _silu(acc)

    qc = conv_silu(q_ref, xq_s, cwq_ref)
    kc = conv_silu(k_ref, xk_s, cwk_ref)
    vc = conv_silu(v_ref, xv_s, cwv_ref)

    def l2(x):
        return x * lax.rsqrt(jnp.sum(x * x, axis=-1, keepdims=True) + NORM_EPS)

    qh = [(l2(qc[:, j * d:(j + 1) * d]) * d ** -0.5).reshape(nc, chunk, d) for j in range(2)]
    kh = [l2(kc[:, j * d:(j + 1) * d]).reshape(nc, chunk, d) for j in range(2)]
    vh = [vc[:, h * d:(h + 1) * d].reshape(nc, chunk, d) for h in range(4)]

    def heads4(per_key_head):
        a, b = per_key_head
        st = jnp.stack([a, a, b, b], axis=1)
        return st.reshape((nc * 4,) + st.shape[2:])

    ba = ba_ref[0]
    beta_all = _sigmoid(ba)
    g_all = -jnp.exp(al_ref[0]) * _softplus(ba + dtb_ref[0])
    mm = lambda a, b: jnp.dot(a, b, preferred_element_type=F32)

    def col4(x, lane0):
        cols = [x[:, lane0 + h:lane0 + h + 1].reshape(nc, chunk, 1) for h in range(4)]
        return jnp.stack(cols, axis=1).reshape(nc * 4, chunk, 1)

    beta = col4(beta_all, 0)
    g_col = col4(g_all, 4)
    nb = nc * 4

    r = lax.broadcasted_iota(I32, (chunk, chunk), 0)
    q_i = lax.broadcasted_iota(I32, (chunk, chunk), 1)
    eye, incl, strict = (r == q_i), (r >= q_i), (r > q_i)
    g_b = jnp.broadcast_to(g_col, (nb, chunk, chunk))
    g_t = jnp.sum(jnp.where(eye, g_b, 0.0), axis=1, keepdims=True)
    gc_t = jnp.sum(jnp.where(r <= q_i, g_b, 0.0), axis=1, keepdims=True)
    gc = jnp.sum(jnp.where(incl, jnp.broadcast_to(g_t, (nb, chunk, chunk)), 0.0), axis=2, keepdims=True)
    decay = jnp.where(incl, jnp.exp(jnp.where(incl, gc - gc_t, 0.0)), 0.0)

    k2 = jnp.concatenate(kh, axis=0).astype(BF16)
    q2 = jnp.concatenate(qh, axis=0).astype(BF16)
    kk = jnp.einsum("bid,bjd->bij", k2, k2, preferred_element_type=F32)
    qk = jnp.einsum("bid,bjd->bij", q2, k2, preferred_element_type=F32)
    kk4 = heads4((kk[:nc], kk[nc:]))
    qk4 = heads4((qk[:nc], qk[nc:]))
    k4 = heads4(kh)
    q4 = heads4(qh)
    v4 = jnp.stack(vh, axis=1).reshape(nb, chunk, d)

    a_mat = jnp.where(strict, kk4 * decay * beta, 0.0)
    attn = (qk4 * decay).astype(BF16)
    t_inv = _unit_lower_inverse(a_mat, chunk)
    eg = jnp.exp(gc)
    g_last = gc[:, chunk - 1:chunk, :]
    rhs = jnp.concatenate([v4 * beta, k4 * (beta * eg)], axis=-1)
    sol = _bmm3(t_inv, rhs)
    u_all, w_all = sol[:, :, :d], sol[:, :, d:].astype(BF16)
    qd_all = (q4 * eg).astype(BF16)
    kt_all = k4 * jnp.exp(g_last - gc)
    egl = jnp.exp(g_last)

    on = on_ref[...]
    s = [state_s[h] for h in range(4)]
    for c in range(nc):
        for h in range(4):
            b = c * 4 + h
            sb = s[h].astype(BF16)
            v_new = u_all[b] - mm(w_all[b], sb)
            o = mm(qd_all[b], sb) + mm(attn[b], v_new.astype(BF16))
            s[h] = s[h] * egl[b] + mm(kt_all[b].T.astype(BF16), v_new.astype(BF16))
            rows = pl.ds(c * chunk, chunk)
            z = z_ref[rows, h * d:(h + 1) * d]
            o_ref[rows, h * d:(h + 1) * d] = (_rms(o) * on * _silu(z)).astype(o_ref.dtype)
    for h in range(4):
        state_s[h] = s[h]

    @pl.when(tt == n_tt - 1)
    def _():
        for h in range(4):
            sfin_ref[0, h] = s[h]


def _gated_deltanet(qkvz, ba_g, al_g, dtb_g, conv_w, conv0, s0, o_norm, o_prev, *, batch, seq, row_off, tt_rows,
                    chunk, key_dim, val_dim, name):
    n_tok = qkvz.shape[0]
    d = HEAD_DIM
    n_g = key_dim // (2 * d)
    assert val_dim == 2 * key_dim and seq % tt_rows == 0 and row_off % tt_rows == 0 and tt_rows % chunk == 0
    n_tt = seq // tt_rows
    rb0 = row_off // tt_rows
    kq, kv_ = key_dim // (2 * d), (2 * key_dim) // (4 * d)

    row = lambda b, g, t: rb0 + b * n_tt + t
    in_specs = [
        pl.BlockSpec((tt_rows, 2 * d), lambda b, g, t: (row(b, g, t), g)),
        pl.BlockSpec((tt_rows, 2 * d), lambda b, g, t: (row(b, g, t), kq + g)),
        pl.BlockSpec((tt_rows, 4 * d), lambda b, g, t: (row(b, g, t), kv_ + g)),
        pl.BlockSpec((tt_rows, 4 * d), lambda b, g, t: (row(b, g, t), 2 * kv_ + g)),
        pl.BlockSpec((1, tt_rows, 8), lambda b, g, t: (g, row(b, g, t), 0)),
        pl.BlockSpec((1, 1, 8), lambda b, g, t: (g, 0, 0)),
        pl.BlockSpec((1, 1, 8), lambda b, g, t: (g, 0, 0)),
        pl.BlockSpec((CONV_W, 2 * d), lambda b, g, t: (0, g)),
        pl.BlockSpec((CONV_W, 2 * d), lambda b, g, t: (0, kq + g)),
        pl.BlockSpec((CONV_W, 4 * d), lambda b, g, t: (0, kv_ + g)),
        pl.BlockSpec((1, CONV_W - 1, 2 * d), lambda b, g, t: (b, 0, g)),
        pl.BlockSpec((1, CONV_W - 1, 2 * d), lambda b, g, t: (b, 0, kq + g)),
        pl.BlockSpec((1, CONV_W - 1, 4 * d), lambda b, g, t: (b, 0, kv_ + g)),
        pl.BlockSpec((1, 4, d, d), lambda b, g, t: (b, g, 0, 0)),
        pl.BlockSpec((1, d), lambda b, g, t: (0, 0)),
    ]
    args = [qkvz, qkvz, qkvz, qkvz, ba_g, al_g, dtb_g, conv_w, conv_w, conv_w, conv0, conv0, conv0, s0,
            o_norm.reshape(1, d)]
    aliases = {}
    if o_prev is not None:
        in_specs.append(pl.BlockSpec(memory_space=pl.ANY))
        args.append(o_prev)
        aliases = {len(args) - 1: 0}
    kern = functools.partial(_gdn_kernel, tt_rows=tt_rows, chunk=chunk)
    if o_prev is not None:
        kern = functools.partial(_drop_arg, kern, len(args) - 1)
    return pl.pallas_call(
        kern,
        grid=(batch, n_g, n_tt),
        in_specs=in_specs,
        out_specs=[pl.BlockSpec((tt_rows, 4 * d), lambda b, g, t: (row(b, g, t), g)),
                   pl.BlockSpec((1, 4, d, d), lambda b, g, t: (b, g, 0, 0))],
        out_shape=[jax.ShapeDtypeStruct((n_tok, val_dim), BF16),
                   jax.ShapeDtypeStruct((batch, 4 * n_g, d, d), F32)],
        scratch_shapes=[pltpu.VMEM((8 + tt_rows, 2 * d), F32), pltpu.VMEM((8 + tt_rows, 2 * d), F32),
                        pltpu.VMEM((8 + tt_rows, 4 * d), F32), pltpu.VMEM((4, d, d), F32)],
        input_output_aliases=aliases,
        compiler_params=_cparams("parallel", "parallel", "arbitrary"),
        name=name,
    )(*args)


def _drop_arg(kern, pos, *refs):
    return kern(*refs[:pos], *refs[pos + 1:])


def _sb_kernel(q_ref, k_ref, v_ref, o_ref, kb_s, vb_s, oacc_s, carry_s, *, tq, tk, q_start, grp):
    qi = pl.program_id(2)
    d = HEAD_DIM
    rows = grp * tq

    @pl.when(qi == 0)
    def _():
        kb_s[...] = k_ref[...].astype(BF16)
        vb_s[...] = v_ref[...].astype(BF16)

    q = q_ref[...]
    qs = jnp.concatenate([q[:, h * d:(h + 1) * d] for h in range(grp)], axis=0)
    oacc_s[...] = jnp.zeros_like(oacc_s)
    carry_s[...] = jnp.zeros_like(carry_s)

    q_lo = q_start + qi * tq
    q_hi = q_lo + tq - 1
    n_kt = (q_hi + tk - 1) // tk
    n_full = q_lo // tk

    uj = lax.broadcasted_iota(I32, (tk, 2 * tk), 0)
    uc = lax.broadcasted_iota(I32, (tk, 2 * tk), 1)
    u = ((uj > uc) | (uc >= tk)).astype(BF16)
    qpos = q_lo + (lax.broadcasted_iota(I32, (rows, tk), 0) & (tq - 1))
    lane = lax.broadcasted_iota(I32, (rows, tk), 1)
    scale = d ** -0.5

    def tile(kt, masked):
        k0 = pl.multiple_of(kt * tk, tk)
        kblk = kb_s[pl.ds(k0, tk), :]
        vblk = vb_s[pl.ds(k0, tk), :]
        z = lax.dot_general(qs, kblk, (((1,), (1,)), ((), ())), preferred_element_type=F32) * scale
        l1p = jnp.log(1.0 + jnp.exp(-jnp.abs(z)))
        log_1m = -(jnp.maximum(z, 0.0) + l1p)
        log_sig = jnp.minimum(z, 0.0) - l1p
        if masked:
            m = (k0 + lane) < qpos
            log_1m = jnp.where(m, log_1m, 0.0)
        hi, lo = _split_bf16(log_1m)
        ext = jnp.dot(hi, u, preferred_element_type=F32) + jnp.dot(lo, u, preferred_element_type=F32)
        after = ext[:, :tk] + carry_s[...]
        a = jnp.exp(log_sig + after)
        if masked:
            a = jnp.where(m, a, 0.0)
        oacc_s[...] += jnp.dot(a.astype(BF16), vblk, preferred_element_type=F32)
        carry_s[...] += ext[:, tk:]

    def masked_body(it, c):
        tile(n_kt - 1 - it, True)
        return c

    def full_body(it, c):
        tile(n_full - 1 - it, False)
        return c

    lax.fori_loop(0, n_kt - n_full, masked_body, 0)
    lax.fori_loop(0, n_full, full_body, 0)
    for h in range(grp):
        o_ref[:, h * d:(h + 1) * d] = oacc_s[h * tq:(h + 1) * tq, :].astype(o_ref.dtype)


def _stick_breaking(q, k_arr, v_arr, o_prev, *, batch, seq, q_row_off, kv_len, kv_row_off, k_col0, v_col0,
                    n_kv, grp, q_start, tq, name):
    n_tok, qd = q.shape
    d = HEAD_DIM
    tk = d
    assert seq % tq == 0 and q_row_off % tq == 0 and kv_row_off % kv_len == 0 and kv_len % tk == 0
    assert tq & (tq - 1) == 0
    assert q_start + seq <= kv_len + tk and (q_start + seq - 1 + tk - 1) // tk * tk <= kv_len
    n_q = seq // tq
    in_specs = [
        pl.BlockSpec((tq, grp * d), lambda b, g, i: (q_row_off // tq + b * n_q + i, g)),
        pl.BlockSpec((kv_len, d), lambda b, g, i: (kv_row_off // kv_len + b, k_col0 + g)),
        pl.BlockSpec((kv_len, d), lambda b, g, i: (kv_row_off // kv_len + b, v_col0 + g)),
    ]
    args = [q, k_arr, v_arr]
    kern = functools.partial(_sb_kernel, tq=tq, tk=tk, q_start=q_start, grp=grp)
    aliases = {}
    if o_prev is not None:
        in_specs.append(pl.BlockSpec(memory_space=pl.ANY))
        args.append(o_prev)
        aliases = {3: 0}
        kern = functools.partial(_drop_arg, kern, 3)
    return pl.pallas_call(
        kern,
        grid=(batch, n_kv, n_q),
        in_specs=in_specs,
        out_specs=pl.BlockSpec((tq, grp * d), lambda b, g, i: (q_row_off // tq + b * n_q + i, g)),
        out_shape=jax.ShapeDtypeStruct((n_tok, qd), BF16),
        scratch_shapes=[pltpu.VMEM((kv_len, d), BF16), pltpu.VMEM((kv_len, d), BF16),
                        pltpu.VMEM((grp * tq, d), F32), pltpu.VMEM((grp * tq, d), F32)],
        input_output_aliases=aliases,
        compiler_params=_cparams("parallel", "parallel", "arbitrary"),
        name=name,
    )(*args)


def _router_kernel(x_ref, sh_ref, sc_ref, rwt_ref, rb_ref, hn_ref, idx_ref, wts_ref, hn_s, *, n_sub):
    for g in range(n_sub):
        rows = pl.ds(g * GROUP_ROWS, GROUP_ROWS)
        hn_s[rows, :] = _rms(x_ref[rows, :]) * (1.0 + sc_ref[g:g + 1, :]) + sh_ref[g:g + 1, :]
    hn = hn_s[...]
    half = hn.shape[1] // 2
    hn_ref[...] = _pack_pair(hn[:, :half], hn[:, half:])

    logits = lax.dot_general(rwt_ref[...], hn, (((1,), (1,)), ((), ())), precision=lax.Precision.HIGHEST,
                             preferred_element_type=F32)
    scores = _sigmoid(logits)
    biased = scores + rb_ref[...]
    n_e, tm = biased.shape
    per = n_e // N_GROUPS
    neg = -jnp.inf
    fsum = lambda x: jnp.sum(x, axis=0, keepdims=True)
    fmax = lambda x: jnp.max(x, axis=0, keepdims=True)

    grp_score = []
    for a in range(N_GROUPS):
        blk = biased[a * per:(a + 1) * per]
        m1 = fmax(blk)
        n_top = fsum((blk == m1).astype(F32))
        m2 = fmax(jnp.where(blk < m1, blk, neg))
        grp_score.append(m1 + jnp.where(n_top > 1.5, m1, m2))
    masked = []
    for a in range(N_GROUPS):
        rank = jnp.zeros_like(grp_score[a])
        for b in range(N_GROUPS):
            if b != a:
                ahead = (grp_score[b] > grp_score[a]) | ((grp_score[b] == grp_score[a]) & (b < a))
                rank = rank + ahead.astype(F32)
        masked.append(jnp.where(rank < TOPK_GROUPS - 0.5, biased[a * per:(a + 1) * per], neg))
    mk = jnp.concatenate(masked, axis=0)
    e_id = lax.broadcasted_iota(I32, (n_e, tm), 0).astype(F32)
    ids, ws = [], []
    for _ in range(TOP_K):
        m = fmax(mk)
        first = jnp.min(jnp.where(mk == m, e_id, float(n_e)), axis=0, keepdims=True)
        hit = e_id == first
        ws.append(fsum(jnp.where(hit, scores, 0.0)))
        ids.append(first)
        mk = jnp.where(hit, neg, mk)
    total = ws[0]
    for w in ws[1:]:
        total = total + w
    idx_ref[...] = jnp.concatenate(ids, axis=0).astype(I32)
    wts_ref[...] = jnp.concatenate([w / total * ROUTED_SCALE for w in ws], axis=0)


def _router(x, sh, sc, rw_t, rb, *, tm):
    n_tok, dm = x.shape
    n_e = rw_t.shape[0]
    n_sub = tm // GROUP_ROWS
    return pl.pallas_call(
        functools.partial(_router_kernel, n_sub=n_sub),
        grid=(n_tok // tm,),
        in_specs=[pl.BlockSpec((tm, dm), lambda i: (i, 0)),
                  pl.BlockSpec((n_sub, dm), lambda i: (i, 0)),
                  pl.BlockSpec((n_sub, dm), lambda i: (i, 0)),
                  pl.BlockSpec((n_e, dm), lambda i: (0, 0)),
                  pl.BlockSpec((n_e, 1), lambda i: (0, 0))],
        out_specs=[pl.BlockSpec((tm, dm // 2), lambda i: (i, 0)),
                   pl.BlockSpec((TOP_K, tm), lambda i: (0, i)),
                   pl.BlockSpec((TOP_K, tm), lambda i: (0, i))],
        out_shape=[jax.ShapeDtypeStruct((n_tok, dm // 2), U32),
                   jax.ShapeDtypeStruct((TOP_K, n_tok), I32),
                   jax.ShapeDtypeStruct((TOP_K, n_tok), F32)],
        scratch_shapes=[pltpu.VMEM((tm, dm), F32)],
        compiler_params=_cparams("parallel"),
        name="moe_router",
    )(x, sh, sc, rw_t, rb.reshape(n_e, 1))


def _expert_kernel(blk_e_ref, src0_ref, src1_ref, slot_ref, hn_hbm, wgu_ref, wd_ref, y_hbm,
                   xbuf, ybuf, gsem, ssem, *, bm):
    del blk_e_ref
    i = pl.program_id(0)
    n_blk = pl.num_programs(0)
    cur = i % 2

    def gather(src_ref, buf):
        for r in range(bm):
            pltpu.make_async_copy(hn_hbm.at[pl.ds(src_ref[0, 0, r], 1)], xbuf.at[buf, pl.ds(r, 1)],
                                  gsem.at[buf]).start()

    def wait_gather(buf):
        pltpu.make_async_copy(hn_hbm.at[pl.ds(0, bm)], xbuf.at[buf], gsem.at[buf]).wait()

    def wait_scatter():
        pltpu.make_async_copy(ybuf, y_hbm.at[pl.ds(0, bm)], ssem.at[0]).wait()

    @pl.when(i == 0)
    def _():
        gather(src0_ref, 0)

    @pl.when(i + 1 < n_blk)
    def _():
        gather(src1_ref, 1 - cur)

    wait_gather(cur)
    lo, hi = _unpack_pair(xbuf[cur])
    half = lo.shape[1]
    gu = (jnp.dot(lo.astype(BF16), wgu_ref[0, :half, :], preferred_element_type=F32)
          + jnp.dot(hi.astype(BF16), wgu_ref[0, half:, :], preferred_element_type=F32))
    de = gu.shape[1] // 2
    h = (_silu(gu[:, :de]) * gu[:, de:]).astype(BF16)
    y = jnp.dot(h, wd_ref[0], preferred_element_type=F32)

    @pl.when(i > 0)
    def _():
        wait_scatter()

    ybuf[...] = _pack_pair(y[:, :half], y[:, half:])
    for r in range(bm):
        pltpu.make_async_copy(ybuf.at[pl.ds(r, 1)], y_hbm.at[pl.ds(slot_ref[0, 0, r], 1)], ssem.at[0]).start()

    @pl.when(i == n_blk - 1)
    def _():
        wait_scatter()


def _experts(hn_packed, wgu, wd, blk_e, src, slot, *, n_slots, bm):
    n_blk = blk_e.shape[0]
    half = hn_packed.shape[1]
    _, dm, de2 = wgu.shape
    src3 = src.reshape(n_blk, 1, bm)
    slot3 = slot.reshape(n_blk, 1, bm)
    smem_blk = lambda f: pl.BlockSpec((1, 1, bm), f, memory_space=pltpu.SMEM)
    grid_spec = pltpu.PrefetchScalarGridSpec(
        num_scalar_prefetch=1,
        grid=(n_blk,),
        in_specs=[smem_blk(lambda i, e: (i, 0, 0)),
                  smem_blk(lambda i, e: (jnp.minimum(i + 1, n_blk - 1), 0, 0)),
                  smem_blk(lambda i, e: (i, 0, 0)),
                  pl.BlockSpec(memory_space=pl.ANY),
                  pl.BlockSpec((1, dm, de2), lambda i, e: (e[i], 0, 0)),
                  pl.BlockSpec((1, de2 // 2, dm), lambda i, e: (e[i], 0, 0))],
        out_specs=pl.BlockSpec(memory_space=pl.ANY),
        scratch_shapes=[pltpu.VMEM((2, bm, half), U32), pltpu.VMEM((bm, half), U32),
                        pltpu.SemaphoreType.DMA((2,)), pltpu.SemaphoreType.DMA((1,))],
    )
    return pl.pallas_call(
        functools.partial(_expert_kernel, bm=bm),
        grid_spec=grid_spec,
        out_shape=jax.ShapeDtypeStruct((n_slots, half), U32),
        compiler_params=_cparams("arbitrary"),
        name="moe_experts",
    )(blk_e, src3, src3, slot3, hn_packed, wgu, wd)


def _combine_kernel(x_ref, y_ref, w_ref, gate_ref, fn_ref, o_ref, ff_s, *, n_sub, final):
    n_k = y_ref.shape[0]
    half = y_ref.shape[2]
    w = w_ref[...]
    lo, hi = _unpack_pair(y_ref[n_k - 1])
    for k in range(n_k - 1):
        yl, yh = _unpack_pair(y_ref[k])
        wk = w[:, k:k + 1]
        lo = lo + wk * yl
        hi = hi + wk * yh
    ff_s[:, :half] = lo
    ff_s[:, half:] = hi
    for g in range(n_sub):
        rows = pl.ds(g * GROUP_ROWS, GROUP_ROWS)
        out = x_ref[rows, :] + gate_ref[g:g + 1, :] * ff_s[rows, :]
        if final:
            out = _rms(out) * fn_ref[...]
        o_ref[rows, :] = out


def _combine(x, y_slots, wts_tok, gate, final_norm, *, tm, final):
    n_tok, dm = x.shape
    n_k = TOP_K + 1
    n_sub = tm // GROUP_ROWS
    y3 = y_slots.reshape(n_k + 1, n_tok, dm // 2)
    return pl.pallas_call(
        functools.partial(_combine_kernel, n_sub=n_sub, final=final),
        grid=(n_tok // tm,),
        in_specs=[pl.BlockSpec((tm, dm), lambda i: (i, 0)),
                  pl.BlockSpec((n_k, tm, dm // 2), lambda i: (0, i, 0)),
                  pl.BlockSpec((tm, TOP_K), lambda i: (i, 0)),
                  pl.BlockSpec((n_sub, dm), lambda i: (i, 0)),
                  pl.BlockSpec((1, dm), lambda i: (0, 0))],
        out_specs=pl.BlockSpec((tm, dm), lambda i: (i, 0)),
        out_shape=jax.ShapeDtypeStruct((n_tok, dm), F32),
        scratch_shapes=[pltpu.VMEM((tm, dm), F32)],
        compiler_params=_cparams("parallel"),
        name="moe_combine",
    )(x, y3, wts_tok, gate, final_norm.reshape(1, dm))


def _moe_plan(idx_t, bm):
    n_tok = idx_t.shape[1]
    n_e = N_EXPERTS + 1
    e_all = jnp.concatenate([idx_t, jnp.full((1, n_tok), N_EXPERTS, I32)], axis=0)
    flat_e = e_all.reshape(-1)
    m = flat_e.shape[0]
    order = jnp.argsort(flat_e).astype(I32)
    sorted_e = flat_e[order]
    counts = jnp.bincount(flat_e, length=n_e).astype(I32)
    padded = (counts + bm - 1) // bm * bm
    pad_end = jnp.cumsum(padded)
    pad_start = pad_end - padded
    start = jnp.cumsum(counts) - counts
    dest = pad_start[sorted_e] + jnp.arange(m, dtype=I32) - start[sorted_e]
    n_blk = -(-(m + n_e * (bm - 1)) // bm)
    p = n_blk * bm
    src = jnp.zeros((p,), I32).at[dest].set(order % n_tok)
    slot = (m + jnp.arange(p, dtype=I32) % bm).at[dest].set(order)
    blk_e = jnp.minimum(jnp.searchsorted(pad_end, jnp.arange(n_blk, dtype=I32) * bm, side="right"), n_e - 1)
    return blk_e.astype(I32), src, slot


def _moe(x, sh, sc, gate, router_w, router_bias, w_gate_up, w_down, ws_gate_up, ws_down, final_norm, *, final, tm):
    n_tok, dm = x.shape
    hn_packed, idx_t, wts_t = _router(x, sh, sc, router_w.T, router_bias, tm=tm)
    blk_e, src, slot = _moe_plan(idx_t, MOE_BM)
    wgu = jnp.concatenate([w_gate_up.astype(BF16), ws_gate_up[None].astype(BF16)], axis=0)
    wd = jnp.concatenate([w_down.astype(BF16), ws_down[None].astype(BF16)], axis=0)
    n_slots = (TOP_K + 2) * n_tok
    y_slots = _experts(hn_packed, wgu, wd, blk_e, src, slot, n_slots=n_slots, bm=MOE_BM)
    return _combine(x, y_slots, wts_t.T, gate, final_norm, tm=tm, final=final)


def _group_rows(mod, bp, tp, bs, ts):
    return jnp.concatenate([jnp.repeat(mod[:bp], tp // GROUP_ROWS, axis=0),
                            jnp.repeat(mod[bp:bp + bs], ts // GROUP_ROWS, axis=0)], axis=0)


def kernel(x_prompt, x_sample, c_prompt, c_sample, state_delta, state_conv, cache_k, cache_v, ada_w, ada_b, a_w_in, a_conv_w, a_a_log, a_dt_bias, a_o_norm, a_w_out, kv_ada_w, kv_ada_b, w_kv, b_w_q, b_w_out, router_w, router_bias, w_gate_up, w_down, ws_gate_up, ws_down, final_norm):
    bp, tp, dm = x_prompt.shape
    bs, ts, _ = x_sample.shape
    n_a = a_w_in.shape[0]
    depth = ada_w.shape[0]
    assert n_a == 1 and depth == 2, "one DeltaNet layer followed by one attention layer"
    assert ts == GROUP_ROWS and tp % GROUP_ROWS == 0
    n_p, n_s = bp * tp, bs * ts
    n_tok = n_p + n_s
    d = HEAD_DIM
    hv = state_delta.shape[2]
    val_dim = hv * d
    key_dim = val_dim // 2
    conv_dim = 2 * key_dim + val_dim
    past = cache_k.shape[1]
    n_kv = cache_k.shape[2]
    grp = b_w_q.shape[2] // (n_kv * d)
    tm_big = 1280 if n_tok % 1280 == 0 else 256
    tm_tok = 256

    x = jnp.concatenate([x_prompt.reshape(n_p, dm), x_sample.reshape(n_s, dm)], axis=0)
    c_rows = 16
    c_all = jnp.concatenate([c_prompt, c_sample, jnp.zeros((c_rows - bp - bs, dm), F32)], axis=0)
    groups = lambda m: _group_rows(m, bp, tp, bs, ts)

    def ada(w, b):
        n = w.shape[1]
        return _linear(c_all, w, tm=c_rows, tn=1024, out_dtype=F32, silu_in=True, bias=b, name="adaln")

    mod = ada(ada_w[0], ada_b[0])
    sh_m, sc_m, gt_m, sh_f, sc_f, gt_f = [groups(m) for m in jnp.split(mod, 6, axis=-1)]
    w_in = a_w_in[0]
    n_main = conv_dim + val_dim
    qkvz = _linear(x, w_in[:, :n_main].astype(BF16), tm=tm_big, tn=1024, out_dtype=F32, norm=True, mod=(sh_m, sc_m),
                   name="gdn_in_proj")
    ba = _linear(x, w_in[:, n_main:].astype(BF16), tm=tm_big, tn=2 * hv, out_dtype=F32, norm=True, mod=(sh_m, sc_m),
                 name="gdn_gate_proj")
    n_g = hv // 4
    ba_g = ba.reshape(n_tok, 2, n_g, 4).transpose(2, 0, 1, 3).reshape(n_g, n_tok, 8)
    pad4 = jnp.zeros((n_g, 4), F32)
    al_g = jnp.concatenate([pad4, a_a_log[0].reshape(n_g, 4)], axis=1).reshape(n_g, 1, 8)
    dtb_g = jnp.concatenate([pad4, a_dt_bias[0].reshape(n_g, 4)], axis=1).reshape(n_g, 1, 8)
    gdn = functools.partial(_gated_deltanet, qkvz, ba_g, al_g, dtb_g, a_conv_w[0], key_dim=key_dim, val_dim=val_dim)
    o_gdn, delta_p = gdn(jnp.zeros((bp, CONV_W - 1, conv_dim), F32), jnp.zeros((bp, hv, d, d), F32), a_o_norm[0], None,
                         batch=bp, seq=tp, row_off=0, tt_rows=512, chunk=CHUNK, name="gdn_prompt")
    o_gdn, delta_s = gdn(state_conv[0], state_delta[0], a_o_norm[0], o_gdn,
                         batch=bs, seq=ts, row_off=n_p, tt_rows=ts, chunk=min(CHUNK, ts), name="gdn_sample")
    conv_p = qkvz[:n_p].reshape(bp, tp, -1)[:, tp - (CONV_W - 1):, :conv_dim]
    conv_s = qkvz[n_p:].reshape(bs, ts, -1)[:, ts - (CONV_W - 1):, :conv_dim]
    x = _linear(o_gdn, a_w_out[0].astype(BF16), tm=tm_big, tn=512, out_dtype=F32, res=x, gate=gt_m, name="gdn_out_proj")
    x = _moe(x, sh_f, sc_f, gt_f, router_w[0], router_bias[0], w_gate_up[0], w_down[0], ws_gate_up[0], ws_down[0],
             final_norm, final=False, tm=tm_tok)

    mod = ada(ada_w[1], ada_b[1])
    sh_m, sc_m, gt_m, sh_f, sc_f, gt_f = [groups(m) for m in jnp.split(mod, 6, axis=-1)]
    kv_mod = ada(kv_ada_w, kv_ada_b)
    kv_sh, kv_sc = [groups(m) for m in jnp.split(kv_mod, 2, axis=-1)]
    kv = _linear(x, w_kv.astype(BF16), tm=tm_big, tn=1024, out_dtype=F32, norm=True, mod=(kv_sh, kv_sc), name="kv_proj")
    q = _linear(x, b_w_q[0].astype(BF16), tm=tm_big, tn=1024, out_dtype=BF16, norm=True, mod=(sh_m, sc_m), name="q_proj")
    kd = n_kv * d
    k_p = kv[:n_p, :kd].reshape(bp, tp, n_kv, d)
    v_p = kv[:n_p, kd:].reshape(bp, tp, n_kv, d)
    k_s = kv[n_p:, :kd].reshape(bs, ts, n_kv, d)
    v_s = kv[n_p:, kd:].reshape(bs, ts, n_kv, d)
    kv_len_s = -(-(past + ts) // d) * d
    tail = jnp.zeros((bs, kv_len_s - past - ts, kd), F32)
    k_all = jnp.concatenate([cache_k.reshape(bs, past, kd), k_s.reshape(bs, ts, kd), tail], axis=1).reshape(-1, kd)
    v_all = jnp.concatenate([cache_v.reshape(bs, past, kd), v_s.reshape(bs, ts, kd), tail], axis=1).reshape(-1, kd)
    o_att = _stick_breaking(q, kv, kv, None, batch=bp, seq=tp, q_row_off=0, kv_len=tp, kv_row_off=0, k_col0=0,
                            v_col0=n_kv, n_kv=n_kv, grp=grp, q_start=0, tq=256, name="attn_prompt")
    o_att = _stick_breaking(q, k_all, v_all, o_att, batch=bs, seq=ts, q_row_off=n_p, kv_len=kv_len_s, kv_row_off=0,
                            k_col0=0, v_col0=0, n_kv=n_kv, grp=grp, q_start=past, tq=ts, name="attn_sample")
    x = _linear(o_att, b_w_out[0].astype(BF16), tm=tm_big, tn=1024, out_dtype=F32, res=x, gate=gt_m, name="attn_out_proj")
    y = _moe(x, sh_f, sc_f, gt_f, router_w[1], router_bias[1], w_gate_up[1], w_down[1], ws_gate_up[1], ws_down[1],
             final_norm, final=True, tm=tm_tok)

    y_p = y[:n_p].reshape(bp, tp, dm)
    y_s = y[n_p:].reshape(bs, ts, dm)
    return (y_p, y_s, k_p, v_p, k_s, v_s, delta_p[None], conv_p[None], delta_s[None], conv_s[None])
```

```python
import functools

import jax
import jax.numpy as jnp
from jax import lax
from jax.experimental import pallas as pl
from jax.experimental.pallas import tpu as pltpu

F32, BF16, I32, U32 = jnp.float32, jnp.bfloat16, jnp.int32, jnp.uint32

NORM_EPS = 1e-6
CHUNK = 64
CONV_W = 4
N_EXPERTS = 64
TOP_K = 8
N_GROUPS = 8
TOPK_GROUPS = 4
ROUTED_SCALE = 2.5
HEAD_DIM = 128
GROUP_ROWS = 32
MOE_BM = 256
SB_KEY_TILE = 256
VMEM_LIMIT_BYTES = 56 * 1024 * 1024


def _cparams(*sem):
    return pltpu.CompilerParams(dimension_semantics=sem, vmem_limit_bytes=VMEM_LIMIT_BYTES)


def _sigmoid(x):
    return 1.0 / (1.0 + jnp.exp(-x))


def _silu(x):
    return x * _sigmoid(x)


def _softplus(x):
    return jnp.maximum(x, 0.0) + jnp.log(1.0 + jnp.exp(-jnp.abs(x)))


def _rms(x):
    return x * lax.rsqrt(jnp.mean(x * x, axis=-1, keepdims=True) + NORM_EPS)


def _pack_pair(lo, hi):
    lo_bits = lax.bitcast_convert_type(lo.astype(BF16).astype(F32), U32) >> 16
    hi_bits = lax.bitcast_convert_type(hi.astype(BF16).astype(F32), U32) & jnp.uint32(0xFFFF0000)
    return lo_bits | hi_bits


def _unpack_pair(w):
    lo = lax.bitcast_convert_type(w << 16, F32)
    hi = lax.bitcast_convert_type(w & jnp.uint32(0xFFFF0000), F32)
    return lo, hi


def _linear_kernel(*refs, norm, silu_in, has_mod, has_bias, has_res, prologue, n_sub):
    it = iter(refs)
    x_ref, w_ref = next(it), next(it)
    sh_ref = sc_ref = b_ref = res_ref = gate_ref = xs_ref = None
    if has_mod:
        sh_ref, sc_ref = next(it), next(it)
    if has_bias:
        b_ref = next(it)
    if has_res:
        res_ref, gate_ref = next(it), next(it)
    o_ref = next(it)
    if prologue:
        xs_ref = next(it)

        @pl.when(pl.program_id(1) == 0)
        def _():
            def prep(x):
                x = x.astype(F32)
                if silu_in:
                    x = _silu(x)
                if norm:
                    x = _rms(x)
                return x

            if has_mod:
                for g in range(n_sub):
                    rows = pl.ds(g * GROUP_ROWS, GROUP_ROWS)
                    xg = prep(x_ref[rows, :])
                    xg = xg * (1.0 + sc_ref[g:g + 1, :]) + sh_ref[g:g + 1, :]
                    xs_ref[rows, :] = xg.astype(BF16)
            else:
                xs_ref[...] = prep(x_ref[...]).astype(BF16)

        xs = xs_ref[...]
    else:
        xs = x_ref[...]
    acc = jnp.dot(xs, w_ref[...].astype(BF16), preferred_element_type=F32)
    if has_bias:
        acc = acc + b_ref[...]
    if has_res:
        for g in range(n_sub):
            rows = pl.ds(g * GROUP_ROWS, GROUP_ROWS)
            part = acc[g * GROUP_ROWS:(g + 1) * GROUP_ROWS, :] * gate_ref[g:g + 1, :]
            o_ref[rows, :] = (res_ref[rows, :] + part).astype(o_ref.dtype)
    else:
        o_ref[...] = acc.astype(o_ref.dtype)


def _linear(x, w, *, tm, tn, out_dtype, norm=False, silu_in=False, mod=None, bias=None, res=None, gate=None,
            name="linear"):
    m, k = x.shape
    n = w.shape[1]
    assert m % tm == 0 and n % tn == 0, (m, tm, n, tn)
    has_mod, has_bias, has_res = mod is not None, bias is not None, res is not None
    prologue = norm or silu_in or has_mod or x.dtype != BF16
    n_sub = tm // GROUP_ROWS if (has_mod or has_res) else 0
    in_specs = [pl.BlockSpec((tm, k), lambda i, j: (i, 0)), pl.BlockSpec((k, tn), lambda i, j: (0, j))]
    args = [x, w]
    if has_mod:
        in_specs += [pl.BlockSpec((n_sub, k), lambda i, j: (i, 0))] * 2
        args += list(mod)
    if has_bias:
        in_specs.append(pl.BlockSpec((1, tn), lambda i, j: (0, j)))
        args.append(bias.reshape(1, n))
    if has_res:
        in_specs += [pl.BlockSpec((tm, tn), lambda i, j: (i, j)), pl.BlockSpec((n_sub, tn), lambda i, j: (i, j))]
        args += [res, gate]
    return pl.pallas_call(
        functools.partial(_linear_kernel, norm=norm, silu_in=silu_in, has_mod=has_mod, has_bias=has_bias,
                          has_res=has_res, prologue=prologue, n_sub=n_sub),
        grid=(m // tm, n // tn),
        in_specs=in_specs,
        out_specs=pl.BlockSpec((tm, tn), lambda i, j: (i, j)),
        out_shape=jax.ShapeDtypeStruct((m, n), out_dtype),
        scratch_shapes=[pltpu.VMEM((tm, k), BF16)] if prologue else [],
        compiler_params=_cparams("parallel", "arbitrary"),
        name=name,
    )(*args)


def _split_bf16(a):
    hi = a.astype(BF16)
    lo = (a - hi.astype(F32)).astype(BF16)
    return hi, lo


def _bmm(a, b):
    return jnp.einsum("bij,bjk->bik", a, b, preferred_element_type=F32)


def _bmm3(a, b):
    ah, al = _split_bf16(a)
    bh, bl = _split_bf16(b)
    m = a.shape[1]
    both = _bmm(jnp.concatenate([ah, al], axis=1), bh)
    return both[:, :m] + both[:, m:] + _bmm(ah, bl)


def _unit_lower_inverse(a, c):
    r = lax.broadcasted_iota(I32, (c, c), 0)
    q = lax.broadcasted_iota(I32, (c, c), 1)
    same = lambda s: (r ^ q) < s
    eye = (r == q).astype(F32)
    a0 = jnp.where(same(8), a, 0.0)
    a2 = _bmm3(a0, a0)
    a4 = _bmm3(a2, a2)
    t = eye - a0
    t = t + _bmm3(t, a2)
    t = t + _bmm3(t, a4)
    s = 8
    while s < c:
        a_off = jnp.where(same(2 * s) & jnp.logical_not(same(s)), a, 0.0)
        t = t - _bmm3(_bmm3(t, a_off), t)
        s *= 2
    return t


def _gdn_kernel(q_ref, k_ref, v_ref, z_ref, ba_ref, al_ref, dtb_ref, cwq_ref, cwk_ref, cwv_ref,
                c0q_ref, c0k_ref, c0v_ref, s0_ref, on_ref, o_ref, sfin_ref,
                xq_s, xk_s, xv_s, state_s, *, tt_rows, chunk):
    tt = pl.program_id(2)
    n_tt = pl.num_programs(2)
    d = HEAD_DIM
    nc = tt_rows // chunk
    halo = CONV_W - 1

    @pl.when(tt == 0)
    def _():
        state_s[...] = s0_ref[0]
        xq_s[8 - halo:8, :] = c0q_ref[0]
        xk_s[8 - halo:8, :] = c0k_ref[0]
        xv_s[8 - halo:8, :] = c0v_ref[0]

    def conv_silu(x_ref, xs, cw_ref):
        xs[8:8 + tt_rows, :] = x_ref[...]
        acc = xs[8 - halo:8 - halo + tt_rows, :] * cw_ref[0:1, :]
        for i in range(1, CONV_W):
            acc = acc + xs[8 - halo + i:8 - halo + i + tt_rows, :] * cw_ref[i:i + 1, :]
        xs[8 - halo:8, :] = xs[8 + tt_rows - halo:8 + tt_rows, :]
        return _silu(acc)

    qc = conv_silu(q_ref, xq_s, cwq_ref)
    kc = conv_silu(k_ref, xk_s, cwk_ref)
    vc = conv_silu(v_ref, xv_s, cwv_ref)

    def l2(x):
        return x * lax.rsqrt(jnp.sum(x * x, axis=-1, keepdims=True) + NORM_EPS)

    qh = [(l2(qc[:, j * d:(j + 1) * d]) * d ** -0.5).reshape(nc, chunk, d) for j in range(2)]
    kh = [l2(kc[:, j * d:(j + 1) * d]).reshape(nc, chunk, d) for j in range(2)]
    vh = [vc[:, h * d:(h + 1) * d].reshape(nc, chunk, d) for h in range(4)]

    def heads4(per_key_head):
        a, b = per_key_head
        st = jnp.stack([a, a, b, b], axis=1)
        return st.reshape((nc * 4,) + st.shape[2:])

    ba = ba_ref[0]
    beta_all = _sigmoid(ba)
    g_all = -jnp.exp(al_ref[0]) * _softplus(ba + dtb_ref[0])
    mm = lambda a, b: jnp.dot(a, b, preferred_element_type=F32)

    def col4(x, lane0):
        cols = [x[:, lane0 + h:lane0 + h + 1].reshape(nc, chunk, 1) for h in range(4)]
        return jnp.stack(cols, axis=1).reshape(nc * 4, chunk, 1)

    beta = col4(beta_all, 0)
    g_col = col4(g_all, 4)
    nb = nc * 4

    r = lax.broadcasted_iota(I32, (chunk, chunk), 0)
    q_i = lax.broadcasted_iota(I32, (chunk, chunk), 1)
    eye, incl, strict = (r == q_i), (r >= q_i), (r > q_i)
    g_b = jnp.broadcast_to(g_col, (nb, chunk, chunk))
    g_t = jnp.sum(jnp.where(eye, g_b, 0.0), axis=1, keepdims=True)
    gc_t = jnp.sum(jnp.where(r <= q_i, g_b, 0.0), axis=1, keepdims=True)
    gc = jnp.sum(jnp.where(incl, jnp.broadcast_to(g_t, (nb, chunk, chunk)), 0.0), axis=2, keepdims=True)
    decay = jnp.where(incl, jnp.exp(jnp.where(incl, gc - gc_t, 0.0)), 0.0)

    k2 = jnp.concatenate(kh, axis=0).astype(BF16)
    q2 = jnp.concatenate(qh, axis=0).astype(BF16)
    kk = jnp.einsum("bid,bjd->bij", k2, k2, preferred_element_type=F32)
    qk = jnp.einsum("bid,bjd->bij", q2, k2, preferred_element_type=F32)
    kk4 = heads4((kk[:nc], kk[nc:]))
    qk4 = heads4((qk[:nc], qk[nc:]))
    k4 = heads4(kh)
    q4 = heads4(qh)
    v4 = jnp.stack(vh, axis=1).reshape(nb, chunk, d)

    a_mat = jnp.where(strict, kk4 * decay * beta, 0.0)
    attn = (qk4 * decay).astype(BF16)
    t_inv = _unit_lower_inverse(a_mat, chunk)
    eg = jnp.exp(gc)
    g_last = gc[:, chunk - 1:chunk, :]
    rhs = jnp.concatenate([v4 * beta, k4 * (beta * eg)], axis=-1)
    sol = _bmm3(t_inv, rhs)
    u_all, w_all = sol[:, :, :d], sol[:, :, d:].astype(BF16)
    qd_all = (q4 * eg).astype(BF16)
    kt_all = k4 * jnp.exp(g_last - gc)
    egl = jnp.exp(g_last)

    on = on_ref[...]
    s = [state_s[h] for h in range(4)]
    for c in range(nc):
        for h in range(4):
            b = c * 4 + h
            sb = s[h].astype(BF16)
            v_new = u_all[b] - mm(w_all[b], sb)
            o = mm(qd_all[b], sb) + mm(attn[b], v_new.astype(BF16))
            s[h] = s[h] * egl[b] + mm(kt_all[b].T.astype(BF16), v_new.astype(BF16))
            rows = pl.ds(c * chunk, chunk)
            z = z_ref[rows, h * d:(h + 1) * d]
            o_ref[rows, h * d:(h + 1) * d] = (_rms(o) * on * _silu(z)).astype(o_ref.dtype)
    for h in range(4):
        state_s[h] = s[h]

    @pl.when(tt == n_tt - 1)
    def _():
        for h in range(4):
            sfin_ref[0, h] = s[h]


def _gated_deltanet(qkvz, ba_g, al_g, dtb_g, conv_w, conv0, s0, o_norm, o_prev, *, batch, seq, row_off, tt_rows,
                    chunk, key_dim, val_dim, name):
    n_tok = qkvz.shape[0]
    d = HEAD_DIM
    n_g = key_dim // (2 * d)
    assert val_dim == 2 * key_dim and seq % tt_rows == 0 and row_off % tt_rows == 0 and tt_rows % chunk == 0
    n_tt = seq // tt_rows
    rb0 = row_off // tt_rows
    kq, kv_ = key_dim // (2 * d), (2 * key_dim) // (4 * d)

    row = lambda b, g, t: rb0 + b * n_tt + t
    in_specs = [
        pl.BlockSpec((tt_rows, 2 * d), lambda b, g, t: (row(b, g, t), g)),
        pl.BlockSpec((tt_rows, 2 * d), lambda b, g, t: (row(b, g, t), kq + g)),
        pl.BlockSpec((tt_rows, 4 * d), lambda b, g, t: (row(b, g, t), kv_ + g)),
        pl.BlockSpec((tt_rows, 4 * d), lambda b, g, t: (row(b, g, t), 2 * kv_ + g)),
        pl.BlockSpec((1, tt_rows, 8), lambda b, g, t: (g, row(b, g, t), 0)),
        pl.BlockSpec((1, 1, 8), lambda b, g, t: (g, 0, 0)),
        pl.BlockSpec((1, 1, 8), lambda b, g, t: (g, 0, 0)),
        pl.BlockSpec((CONV_W, 2 * d), lambda b, g, t: (0, g)),
        pl.BlockSpec((CONV_W, 2 * d), lambda b, g, t: (0, kq + g)),
        pl.BlockSpec((CONV_W, 4 * d), lambda b, g, t: (0, kv_ + g)),
        pl.BlockSpec((1, CONV_W - 1, 2 * d), lambda b, g, t: (b, 0, g)),
        pl.BlockSpec((1, CONV_W - 1, 2 * d), lambda b, g, t: (b, 0, kq + g)),
        pl.BlockSpec((1, CONV_W - 1, 4 * d), lambda b, g, t: (b, 0, kv_ + g)),
        pl.BlockSpec((1, 4, d, d), lambda b, g, t: (b, g, 0, 0)),
        pl.BlockSpec((1, d), lambda b, g, t: (0, 0)),
    ]
    args = [qkvz, qkvz, qkvz, qkvz, ba_g, al_g, dtb_g, conv_w, conv_w, conv_w, conv0, conv0, conv0, s0,
            o_norm.reshape(1, d)]
    aliases = {}
    if o_prev is not None:
        in_specs.append(pl.BlockSpec(memory_space=pl.ANY))
        args.append(o_prev)
        aliases = {len(args) - 1: 0}
    kern = functools.partial(_gdn_kernel, tt_rows=tt_rows, chunk=chunk)
    if o_prev is not None:
        kern = functools.partial(_drop_arg, kern, len(args) - 1)
    return pl.pallas_call(
        kern,
        grid=(batch, n_g, n_tt),
        in_specs=in_specs,
        out_specs=[pl.BlockSpec((tt_rows, 4 * d), lambda b, g, t: (row(b, g, t), g)),
                   pl.BlockSpec((1, 4, d, d), lambda b, g, t: (b, g, 0, 0))],
        out_shape=[jax.ShapeDtypeStruct((n_tok, val_dim), BF16),
                   jax.ShapeDtypeStruct((batch, 4 * n_g, d, d), F32)],
        scratch_shapes=[pltpu.VMEM((8 + tt_rows, 2 * d), F32), pltpu.VMEM((8 + tt_rows, 2 * d), F32),
                        pltpu.VMEM((8 + tt_rows, 4 * d), F32), pltpu.VMEM((4, d, d), F32)],
        input_output_aliases=aliases,
        compiler_params=_cparams("parallel", "parallel", "arbitrary"),
        name=name,
    )(*args)


def _drop_arg(kern, pos, *refs):
    return kern(*refs[:pos], *refs[pos + 1:])


def _sb_kernel(q_ref, k_ref, v_ref, o_ref, kb_s, vb_s, oacc_s, carry_s, *, tq, tk, q_start, grp):
    qi = pl.program_id(2)
    d = HEAD_DIM
    rows = grp * tq

    @pl.when(qi == 0)
    def _():
        kb_s[...] = k_ref[...].astype(BF16)
        vb_s[...] = v_ref[...].astype(BF16)

    q = q_ref[...]
    qs = jnp.concatenate([q[:, h * d:(h + 1) * d] for h in range(grp)], axis=0)
    oacc_s[...] = jnp.zeros_like(oacc_s)
    carry_s[...] = jnp.zeros_like(carry_s)

    q_lo = q_start + qi * tq
    q_hi = q_lo + tq - 1
    n_kt = (q_hi + tk - 1) // tk
    n_full = q_lo // tk

    u = (lax.broadcasted_iota(I32, (tk, tk), 0) > lax.broadcasted_iota(I32, (tk, tk), 1)).astype(BF16)
    qpos = q_lo + (lax.broadcasted_iota(I32, (rows, tk), 0) & (tq - 1))
    lane = lax.broadcasted_iota(I32, (rows, tk), 1)
    c2 = d ** -0.5 * 1.4426950408889634
    sign = jnp.uint32(0x80000000)

    def tile(kt, masked):
        k0 = pl.multiple_of(kt * tk, tk)
        kblk = kb_s[pl.ds(k0, tk), :]
        vblk = vb_s[pl.ds(k0, tk), :]
        z = lax.dot_general(qs, kblk, (((1,), (1,)), ((), ())), preferred_element_type=F32) * c2
        neg_abs = lax.bitcast_convert_type(lax.bitcast_convert_type(z, U32) | sign, F32)
        t = jnp.maximum(z, 0.0) + jnp.log2(1.0 + jnp.exp2(neg_abs))
        if masked:
            m = (k0 + lane) < qpos
            t = jnp.where(m, t, 0.0)
        hi, lo = _split_bf16(t)
        later = jnp.dot(hi, u, preferred_element_type=F32) + jnp.dot(lo, u, preferred_element_type=F32)
        carry = carry_s[...]
        a = jnp.exp2((z - t) - later - jnp.concatenate([carry] * (tk // d), axis=1))
        if masked:
            a = jnp.where(m, a, 0.0)
        oacc_s[...] += jnp.dot(a.astype(BF16), vblk, preferred_element_type=F32)
        carry_s[...] = carry + jnp.broadcast_to(later[:, 0:1] + t[:, 0:1], carry.shape)

    def masked_body(it, c):
        tile(n_kt - 1 - it, True)
        return c

    def full_body(it, c):
        tile(n_full - 1 - it, False)
        return c

    lax.fori_loop(0, n_kt - n_full, masked_body, 0)
    lax.fori_loop(0, n_full, full_body, 0)
    for h in range(grp):
        o_ref[:, h * d:(h + 1) * d] = oacc_s[h * tq:(h + 1) * tq, :].astype(o_ref.dtype)


def _stick_breaking(q, k_arr, v_arr, o_prev, *, batch, seq, q_row_off, kv_len, kv_row_off, k_col0, v_col0,
                    n_kv, grp, q_start, tq, name):
    n_tok, qd = q.shape
    d = HEAD_DIM
    tk = SB_KEY_TILE
    assert seq % tq == 0 and q_row_off % tq == 0 and kv_row_off % kv_len == 0 and kv_len % tk == 0
    assert tq & (tq - 1) == 0
    assert q_start + seq <= kv_len + tk and (q_start + seq - 1 + tk - 1) // tk * tk <= kv_len
    n_q = seq // tq
    in_specs = [
        pl.BlockSpec((tq, grp * d), lambda b, g, i: (q_row_off // tq + b * n_q + i, g)),
        pl.BlockSpec((kv_len, d), lambda b, g, i: (kv_row_off // kv_len + b, k_col0 + g)),
        pl.BlockSpec((kv_len, d), lambda b, g, i: (kv_row_off // kv_len + b, v_col0 + g)),
    ]
    args = [q, k_arr, v_arr]
    kern = functools.partial(_sb_kernel, tq=tq, tk=tk, q_start=q_start, grp=grp)
    aliases = {}
    if o_prev is not None:
        in_specs.append(pl.BlockSpec(memory_space=pl.ANY))
        args.append(o_prev)
        aliases = {3: 0}
        kern = functools.partial(_drop_arg, kern, 3)
    return pl.pallas_call(
        kern,
        grid=(batch, n_kv, n_q),
        in_specs=in_specs,
        out_specs=pl.BlockSpec((tq, grp * d), lambda b, g, i: (q_row_off // tq + b * n_q + i, g)),
        out_shape=jax.ShapeDtypeStruct((n_tok, qd), BF16),
        scratch_shapes=[pltpu.VMEM((kv_len, d), BF16), pltpu.VMEM((kv_len, d), BF16),
                        pltpu.VMEM((grp * tq, d), F32), pltpu.VMEM((grp * tq, d), F32)],
        input_output_aliases=aliases,
        compiler_params=_cparams("parallel", "parallel", "arbitrary"),
        name=name,
    )(*args)


def _router_kernel(x_ref, sh_ref, sc_ref, rwt_ref, rb_ref, hn_ref, idx_ref, wts_ref, hn_s, *, n_sub):
    for g in range(n_sub):
        rows = pl.ds(g * GROUP_ROWS, GROUP_ROWS)
        hn_s[rows, :] = _rms(x_ref[rows, :]) * (1.0 + sc_ref[g:g + 1, :]) + sh_ref[g:g + 1, :]
    hn = hn_s[...]
    half = hn.shape[1] // 2
    hn_ref[...] = _pack_pair(hn[:, :half], hn[:, half:])

    logits = lax.dot_general(rwt_ref[...], hn, (((1,), (1,)), ((), ())), precision=lax.Precision.HIGHEST,
                             preferred_element_type=F32)
    scores = _sigmoid(logits)
    biased = scores + rb_ref[...]
    n_e, tm = biased.shape
    per = n_e // N_GROUPS
    neg = -jnp.inf
    fsum = lambda x: jnp.sum(x, axis=0, keepdims=True)
    fmax = lambda x: jnp.max(x, axis=0, keepdims=True)

    grp_score = []
    for a in range(N_GROUPS):
        blk = biased[a * per:(a + 1) * per]
        m1 = fmax(blk)
        n_top = fsum((blk == m1).astype(F32))
        m2 = fmax(jnp.where(blk < m1, blk, neg))
        grp_score.append(m1 + jnp.where(n_top > 1.5, m1, m2))
    masked = []
    for a in range(N_GROUPS):
        rank = jnp.zeros_like(grp_score[a])
        for b in range(N_GROUPS):
            if b != a:
                ahead = (grp_score[b] > grp_score[a]) | ((grp_score[b] == grp_score[a]) & (b < a))
                rank = rank + ahead.astype(F32)
        masked.append(jnp.where(rank < TOPK_GROUPS - 0.5, biased[a * per:(a + 1) * per], neg))
    mk = jnp.concatenate(masked, axis=0)
    e_id = lax.broadcasted_iota(I32, (n_e, tm), 0).astype(F32)
    ids, ws = [], []
    for _ in range(TOP_K):
        m = fmax(mk)
        first = jnp.min(jnp.where(mk == m, e_id, float(n_e)), axis=0, keepdims=True)
        hit = e_id == first
        ws.append(fsum(jnp.where(hit, scores, 0.0)))
        ids.append(first)
        mk = jnp.where(hit, neg, mk)
    total = ws[0]
    for w in ws[1:]:
        total = total + w
    idx_ref[...] = jnp.concatenate(ids, axis=0).astype(I32)
    wts_ref[...] = jnp.concatenate([w / total * ROUTED_SCALE for w in ws], axis=0)


def _router(x, sh, sc, rw_t, rb, *, tm):
    n_tok, dm = x.shape
    n_e = rw_t.shape[0]
    n_sub = tm // GROUP_ROWS
    return pl.pallas_call(
        functools.partial(_router_kernel, n_sub=n_sub),
        grid=(n_tok // tm,),
        in_specs=[pl.BlockSpec((tm, dm), lambda i: (i, 0)),
                  pl.BlockSpec((n_sub, dm), lambda i: (i, 0)),
                  pl.BlockSpec((n_sub, dm), lambda i: (i, 0)),
                  pl.BlockSpec((n_e, dm), lambda i: (0, 0)),
                  pl.BlockSpec((n_e, 1), lambda i: (0, 0))],
        out_specs=[pl.BlockSpec((tm, dm // 2), lambda i: (i, 0)),
                   pl.BlockSpec((TOP_K, tm), lambda i: (0, i)),
                   pl.BlockSpec((TOP_K, tm), lambda i: (0, i))],
        out_shape=[jax.ShapeDtypeStruct((n_tok, dm // 2), U32),
                   jax.ShapeDtypeStruct((TOP_K, n_tok), I32),
                   jax.ShapeDtypeStruct((TOP_K, n_tok), F32)],
        scratch_shapes=[pltpu.VMEM((tm, dm), F32)],
        compiler_params=_cparams("parallel"),
        name="moe_router",
    )(x, sh, sc, rw_t, rb.reshape(n_e, 1))


IDX_ALIGN = 128
CAST_ROWS = 256
SCATTER_SHIFT = 3
SCATTER_UNROLL = 1 << SCATTER_SHIFT


def _expert_kernel(blk_e_ref, s0_ref, cnt_ref, tok_hbm, slot_hbm, hn_hbm, wgu_ref, wd_ref, *rest, bm, has_prev):
    y_hbm, tbuf, sbuf, xbuf, ybuf, wgu_s, wd_s, gsem, ssem, isem = rest[1:] if has_prev else rest
    i = pl.program_id(0)
    n_blk = pl.num_programs(0)
    cur = i % 2
    win = bm + IDX_ALIGN

    def idx_copies(b):
        a0 = pl.multiple_of(s0_ref[b] & (-IDX_ALIGN), IDX_ALIGN)
        return (pltpu.make_async_copy(tok_hbm.at[pl.ds(a0, win)], tbuf.at[b % 3], isem.at[0, b % 3]),
                pltpu.make_async_copy(slot_hbm.at[pl.ds(a0, win)], sbuf.at[b % 3], isem.at[1, b % 3]))

    def idx_start(b):
        for c in idx_copies(b):
            c.start()

    def idx_wait(b):
        for c in idx_copies(b):
            c.wait()

    def gather(b, buf):
        buf3 = b % 3
        off = s0_ref[b] & (IDX_ALIGN - 1)
        for r in range(bm):
            pltpu.make_async_copy(hn_hbm.at[pl.ds(tbuf[buf3, off + r], 1)], xbuf.at[buf, pl.ds(r, 1)],
                                  gsem.at[buf]).start()

    def wait_gather(buf):
        pltpu.make_async_copy(hn_hbm.at[pl.ds(0, bm)], xbuf.at[buf], gsem.at[buf]).wait()

    def scatter_row(buf3, off, r):
        pltpu.make_async_copy(ybuf.at[pl.ds(r, 1)], y_hbm.at[pl.ds(sbuf[buf3, off + r], 1)], ssem.at[0]).start()

    def wait_scatter(n_rows):
        n8 = pl.multiple_of((n_rows >> SCATTER_SHIFT) << SCATTER_SHIFT, SCATTER_UNROLL)

        @pl.when(n8 > 0)
        def _():
            pltpu.make_async_copy(ybuf.at[pl.ds(0, n8)], y_hbm.at[pl.ds(0, n8)], ssem.at[0]).wait()

        def one_row(r, c):
            pltpu.make_async_copy(ybuf.at[pl.ds(0, 1)], y_hbm.at[pl.ds(0, 1)], ssem.at[0]).wait()
            return c

        lax.fori_loop(n8, n_rows, one_row, 0)

    @pl.when(i == 0)
    def _():
        idx_start(0)
        idx_wait(0)
        gather(0, 0)

        @pl.when(n_blk > 1)
        def _():
            idx_start(1)

    @pl.when(i + 1 < n_blk)
    def _():
        idx_wait(i + 1)
        gather(i + 1, 1 - cur)

    @pl.when(i + 2 < n_blk)
    def _():
        idx_start(i + 2)

    @pl.when((i == 0) | (blk_e_ref[i] != blk_e_ref[jnp.maximum(i - 1, 0)]))
    def _():
        for dst, src in ((wgu_s, wgu_ref), (wd_s, wd_ref)):
            step = min(CAST_ROWS, dst.shape[0])
            for r0 in range(0, dst.shape[0], step):
                dst[r0:r0 + step, :] = src[0, r0:r0 + step, :].astype(BF16)

    wait_gather(cur)
    lo, hi = _unpack_pair(xbuf[cur])
    half = lo.shape[1]
    gu = (jnp.dot(lo.astype(BF16), wgu_s[:half, :], preferred_element_type=F32)
          + jnp.dot(hi.astype(BF16), wgu_s[half:, :], preferred_element_type=F32))
    de = gu.shape[1] // 2
    h = (_silu(gu[:, :de]) * gu[:, de:]).astype(BF16)
    y = jnp.dot(h, wd_s[...], preferred_element_type=F32)

    @pl.when(i > 0)
    def _():
        wait_scatter(cnt_ref[jnp.maximum(i - 1, 0)])

    ybuf[...] = _pack_pair(y[:, :half], y[:, half:])
    cnt = cnt_ref[i]
    buf3 = i % 3
    off = s0_ref[i] & (IDX_ALIGN - 1)
    n_grp = cnt >> SCATTER_SHIFT

    def grp_body(j, c):
        for u in range(SCATTER_UNROLL):
            scatter_row(buf3, off, j * SCATTER_UNROLL + u)
        return c

    def row_body(r, c):
        scatter_row(buf3, off, r)
        return c

    lax.fori_loop(0, n_grp, grp_body, 0)
    lax.fori_loop(n_grp * SCATTER_UNROLL, cnt, row_body, 0)

    @pl.when(i == n_blk - 1)
    def _():
        wait_scatter(cnt)


def _experts(hn_packed, wgu, wd, plan, y_prev, *, n_slots, bm, name):
    blk_e, s0, cnt, tok, slot = plan
    n_blk = blk_e.shape[0]
    n_tok, half = hn_packed.shape
    _, dm, de2 = wgu.shape
    any_spec = pl.BlockSpec(memory_space=pl.ANY)
    in_specs = [any_spec, any_spec, any_spec,
                pl.BlockSpec((1, dm, de2), lambda i, e, s, c: (e[i], 0, 0)),
                pl.BlockSpec((1, de2 // 2, dm), lambda i, e, s, c: (e[i], 0, 0))]
    args = [blk_e, s0, cnt, tok, slot, hn_packed, wgu, wd]
    aliases = {}
    if y_prev is not None:
        in_specs.append(any_spec)
        args.append(y_prev)
        aliases = {len(args) - 1: 0}
    grid_spec = pltpu.PrefetchScalarGridSpec(
        num_scalar_prefetch=3,
        grid=(n_blk,),
        in_specs=in_specs,
        out_specs=any_spec,
        scratch_shapes=[pltpu.SMEM((3, bm + IDX_ALIGN), I32), pltpu.SMEM((3, bm + IDX_ALIGN), I32),
                        pltpu.VMEM((2, bm, half), U32), pltpu.VMEM((bm, half), U32),
                        pltpu.VMEM((dm, de2), BF16), pltpu.VMEM((de2 // 2, dm), BF16),
                        pltpu.SemaphoreType.DMA((2,)), pltpu.SemaphoreType.DMA((1,)), pltpu.SemaphoreType.DMA((2, 3))],
    )
    return pl.pallas_call(
        functools.partial(_expert_kernel, bm=bm, has_prev=y_prev is not None),
        grid_spec=grid_spec,
        out_shape=jax.ShapeDtypeStruct((n_slots, half), U32),
        input_output_aliases=aliases,
        compiler_params=_cparams("arbitrary"),
        name=name,
    )(*args)


def _combine_kernel(x_ref, y_ref, w_ref, gate_ref, fn_ref, *rest, n_sub, final, n_first):
    outs, ff_s = rest[:-1], rest[-1]
    n_k = y_ref.shape[0]
    half = y_ref.shape[2]
    w = w_ref[...]
    lo, hi = _unpack_pair(y_ref[n_k - 1])
    for k in range(n_k - 1):
        yl, yh = _unpack_pair(y_ref[k])
        wk = w[:, k:k + 1]
        lo = lo + wk * yl
        hi = hi + wk * yh
    ff_s[:, :half] = lo
    ff_s[:, half:] = hi
    for g in range(n_sub):
        rows = pl.ds(g * GROUP_ROWS, GROUP_ROWS)
        out = x_ref[rows, :] + gate_ref[g:g + 1, :] * ff_s[rows, :]
        if final:
            out = _rms(out) * fn_ref[...]
        ff_s[rows, :] = out
    if n_first is None:
        outs[0][...] = ff_s[...]
    else:
        i = pl.program_id(0)

        @pl.when(i < n_first)
        def _():
            outs[0][...] = ff_s[...]

        @pl.when(i >= n_first)
        def _():
            outs[1][...] = ff_s[...]


def _combine(x, y_slots, wts_tok, gate, final_norm, *, tm, final, n_first_rows=None):
    n_tok, dm = x.shape
    n_k = TOP_K + 1
    n_sub = tm // GROUP_ROWS
    y3 = y_slots.reshape(n_k, n_tok, dm // 2)
    if n_first_rows is None:
        n_first = None
        out_specs = pl.BlockSpec((tm, dm), lambda i: (i, 0))
        out_shape = jax.ShapeDtypeStruct((n_tok, dm), F32)
    else:
        assert n_first_rows % tm == 0 and n_tok - n_first_rows == tm
        n_first = n_first_rows // tm
        out_specs = [pl.BlockSpec((tm, dm), lambda i: (jnp.minimum(i, n_first - 1), 0)),
                     pl.BlockSpec((tm, dm), lambda i: (0, 0))]
        out_shape = [jax.ShapeDtypeStruct((n_first_rows, dm), F32), jax.ShapeDtypeStruct((tm, dm), F32)]
    return pl.pallas_call(
        functools.partial(_combine_kernel, n_sub=n_sub, final=final, n_first=n_first),
        grid=(n_tok // tm,),
        in_specs=[pl.BlockSpec((tm, dm), lambda i: (i, 0)),
                  pl.BlockSpec((n_k, tm, dm // 2), lambda i: (0, i, 0)),
                  pl.BlockSpec((tm, TOP_K), lambda i: (i, 0)),
                  pl.BlockSpec((n_sub, dm), lambda i: (i, 0)),
                  pl.BlockSpec((1, dm), lambda i: (0, 0))],
        out_specs=out_specs,
        out_shape=out_shape,
        scratch_shapes=[pltpu.VMEM((tm, dm), F32)],
        compiler_params=_cparams("arbitrary"),
        name="moe_combine",
    )(x, y3, wts_tok, gate, final_norm.reshape(1, dm))


def _moe_plan(idx_t, bm):
    n_tok = idx_t.shape[1]
    assert n_tok < (1 << 16)
    flat_e = idx_t.reshape(-1)
    a = flat_e.shape[0]
    _, order = lax.sort((flat_e, jnp.arange(a, dtype=I32)), num_keys=1, is_stable=True)
    tail = jnp.zeros((bm + 2 * IDX_ALIGN,), I32)
    tok = jnp.concatenate([order % n_tok, tail])
    slot = jnp.concatenate([order, tail])
    e_ids = jnp.arange(N_EXPERTS, dtype=I32)
    counts = jnp.sum((flat_e[:, None] == e_ids[None, :]).astype(I32), axis=0)
    start = jnp.cumsum(counts) - counts
    nblk_e = (counts + bm - 1) // bm
    blk_end = jnp.cumsum(nblk_e)
    n_blk = -(-(a + N_EXPERTS * (bm - 1)) // bm)
    b = jnp.arange(n_blk, dtype=I32)
    blk_e = jnp.minimum(jnp.sum((b[:, None] >= blk_end[None, :]).astype(I32), axis=1), N_EXPERTS - 1)
    onehot = (blk_e[:, None] == e_ids[None, :]).astype(I32)
    pick = lambda v: jnp.sum(onehot * v[None, :], axis=1)
    j = b - pick(blk_end - nblk_e)
    s0 = jnp.minimum(pick(start) + j * bm, a)
    cnt = jnp.clip(pick(counts) - j * bm, 0, bm)
    return blk_e, s0, cnt, tok, slot


def _shared_plan(n_tok, bm):
    assert n_tok % bm == 0
    n_blk = n_tok // bm
    tail = jnp.zeros((bm + 2 * IDX_ALIGN,), I32)
    tok = jnp.concatenate([jnp.arange(n_tok, dtype=I32), tail])
    slot = jnp.concatenate([TOP_K * n_tok + jnp.arange(n_tok, dtype=I32), tail])
    return (jnp.zeros((n_blk,), I32), jnp.arange(n_blk, dtype=I32) * bm, jnp.full((n_blk,), bm, I32), tok, slot)


def _moe(x, sh, sc, gate, router_w, router_bias, w_gate_up, w_down, ws_gate_up, ws_down, final_norm, *, final, tm,
         n_first_rows=None):
    n_tok, dm = x.shape
    hn_packed, idx_t, wts_t = _router(x, sh, sc, router_w.T, router_bias, tm=tm)
    n_slots = (TOP_K + 1) * n_tok
    y_slots = _experts(hn_packed, w_gate_up, w_down, _moe_plan(idx_t, MOE_BM), None, n_slots=n_slots,
                       bm=MOE_BM, name="moe_experts")
    y_slots = _experts(hn_packed, ws_gate_up[None], ws_down[None], _shared_plan(n_tok, MOE_BM), y_slots,
                       n_slots=n_slots, bm=MOE_BM, name="moe_shared_expert")
    return _combine(x, y_slots, wts_t.T, gate, final_norm, tm=tm, final=final, n_first_rows=n_first_rows)


def _group_rows(mod, bp, tp, bs, ts):
    return jnp.concatenate([jnp.repeat(mod[:bp], tp // GROUP_ROWS, axis=0),
                            jnp.repeat(mod[bp:bp + bs], ts // GROUP_ROWS, axis=0)], axis=0)


def kernel(x_prompt, x_sample, c_prompt, c_sample, state_delta, state_conv, cache_k, cache_v, ada_w, ada_b, a_w_in, a_conv_w, a_a_log, a_dt_bias, a_o_norm, a_w_out, kv_ada_w, kv_ada_b, w_kv, b_w_q, b_w_out, router_w, router_bias, w_gate_up, w_down, ws_gate_up, ws_down, final_norm):
    bp, tp, dm = x_prompt.shape
    bs, ts, _ = x_sample.shape
    n_a = a_w_in.shape[0]
    depth = ada_w.shape[0]
    assert n_a == 1 and depth == 2, "one DeltaNet layer followed by one attention layer"
    assert ts == GROUP_ROWS and tp % GROUP_ROWS == 0
    n_p, n_s = bp * tp, bs * ts
    n_tok = n_p + n_s
    d = HEAD_DIM
    hv = state_delta.shape[2]
    val_dim = hv * d
    key_dim = val_dim // 2
    conv_dim = 2 * key_dim + val_dim
    past = cache_k.shape[1]
    n_kv = cache_k.shape[2]
    grp = b_w_q.shape[2] // (n_kv * d)
    tm_big = 1280 if n_tok % 1280 == 0 else 256
    tm_tok = 256

    x = jnp.concatenate([x_prompt.reshape(n_p, dm), x_sample.reshape(n_s, dm)], axis=0)
    c_rows = 16
    c_all = jnp.concatenate([c_prompt, c_sample, jnp.zeros((c_rows - bp - bs, dm), F32)], axis=0)
    groups = lambda m: _group_rows(m, bp, tp, bs, ts)

    def ada(w, b):
        n = w.shape[1]
        return _linear(c_all, w, tm=c_rows, tn=1024, out_dtype=F32, silu_in=True, bias=b, name="adaln")

    mod = ada(ada_w[0], ada_b[0])
    sh_m, sc_m, gt_m, sh_f, sc_f, gt_f = [groups(m) for m in jnp.split(mod, 6, axis=-1)]
    w_in = a_w_in[0]
    n_main = conv_dim + val_dim
    qkvz = _linear(x, w_in[:, :n_main].astype(BF16), tm=tm_big, tn=1024, out_dtype=F32, norm=True, mod=(sh_m, sc_m),
                   name="gdn_in_proj")
    ba = _linear(x, w_in[:, n_main:].astype(BF16), tm=tm_big, tn=2 * hv, out_dtype=F32, norm=True, mod=(sh_m, sc_m),
                 name="gdn_gate_proj")
    n_g = hv // 4
    ba_g = ba.reshape(n_tok, 2, n_g, 4).transpose(2, 0, 1, 3).reshape(n_g, n_tok, 8)
    pad4 = jnp.zeros((n_g, 4), F32)
    al_g = jnp.concatenate([pad4, a_a_log[0].reshape(n_g, 4)], axis=1).reshape(n_g, 1, 8)
    dtb_g = jnp.concatenate([pad4, a_dt_bias[0].reshape(n_g, 4)], axis=1).reshape(n_g, 1, 8)
    gdn = functools.partial(_gated_deltanet, qkvz, ba_g, al_g, dtb_g, a_conv_w[0], key_dim=key_dim, val_dim=val_dim)
    o_gdn, delta_p = gdn(jnp.zeros((bp, CONV_W - 1, conv_dim), F32), jnp.zeros((bp, hv, d, d), F32), a_o_norm[0], None,
                         batch=bp, seq=tp, row_off=0, tt_rows=512, chunk=CHUNK, name="gdn_prompt")
    o_gdn, delta_s = gdn(state_conv[0], state_delta[0], a_o_norm[0], o_gdn,
                         batch=bs, seq=ts, row_off=n_p, tt_rows=ts, chunk=min(CHUNK, ts), name="gdn_sample")
    last_rows = lambda r0, t: lax.slice(qkvz, (r0 + t - (CONV_W - 1), 0), (r0 + t, conv_dim))
    conv_p = jnp.stack([last_rows(b * tp, tp) for b in range(bp)])
    conv_s = jnp.stack([last_rows(n_p + b * ts, ts) for b in range(bs)])
    x = _linear(o_gdn, a_w_out[0].astype(BF16), tm=tm_big, tn=512, out_dtype=F32, res=x, gate=gt_m, name="gdn_out_proj")
    x = _moe(x, sh_f, sc_f, gt_f, router_w[0], router_bias[0], w_gate_up[0], w_down[0], ws_gate_up[0], ws_down[0],
             final_norm, final=False, tm=tm_tok)

    mod = ada(ada_w[1], ada_b[1])
    sh_m, sc_m, gt_m, sh_f, sc_f, gt_f = [groups(m) for m in jnp.split(mod, 6, axis=-1)]
    kv_mod = ada(kv_ada_w, kv_ada_b)
    kv_sh, kv_sc = [groups(m) for m in jnp.split(kv_mod, 2, axis=-1)]
    kv = _linear(x, w_kv.astype(BF16), tm=tm_big, tn=1024, out_dtype=F32, norm=True, mod=(kv_sh, kv_sc), name="kv_proj")
    q = _linear(x, b_w_q[0].astype(BF16), tm=tm_big, tn=1024, out_dtype=BF16, norm=True, mod=(sh_m, sc_m), name="q_proj")
    kd = n_kv * d
    k_p = kv[:n_p, :kd].reshape(bp, tp, n_kv, d)
    v_p = kv[:n_p, kd:].reshape(bp, tp, n_kv, d)
    k_s = kv[n_p:, :kd].reshape(bs, ts, n_kv, d)
    v_s = kv[n_p:, kd:].reshape(bs, ts, n_kv, d)
    kv_len_s = -(-(past + ts) // SB_KEY_TILE) * SB_KEY_TILE
    tail = jnp.zeros((bs, kv_len_s - past - ts, kd), F32)
    k_all = jnp.concatenate([cache_k.reshape(bs, past, kd), k_s.reshape(bs, ts, kd), tail], axis=1).reshape(-1, kd)
    v_all = jnp.concatenate([cache_v.reshape(bs, past, kd), v_s.reshape(bs, ts, kd), tail], axis=1).reshape(-1, kd)
    o_att = _stick_breaking(q, kv, kv, None, batch=bp, seq=tp, q_row_off=0, kv_len=tp, kv_row_off=0, k_col0=0,
                            v_col0=n_kv, n_kv=n_kv, grp=grp, q_start=0, tq=256, name="attn_prompt")
    o_att = _stick_breaking(q, k_all, v_all, o_att, batch=bs, seq=ts, q_row_off=n_p, kv_len=kv_len_s, kv_row_off=0,
                            k_col0=0, v_col0=0, n_kv=n_kv, grp=grp, q_start=past, tq=ts, name="attn_sample")
    x = _linear(o_att, b_w_out[0].astype(BF16), tm=tm_big, tn=1024, out_dtype=F32, res=x, gate=gt_m, name="attn_out_proj")
    assert n_s == tm_tok
    y_p, y_s = _moe(x, sh_f, sc_f, gt_f, router_w[1], router_bias[1], w_gate_up[1], w_down[1], ws_gate_up[1],
                    ws_down[1], final_norm, final=True, tm=tm_tok, n_first_rows=n_p)
    y_p = y_p.reshape(bp, tp, dm)
    y_s = y_s.reshape(bs, ts, dm)
    return (y_p, y_s, k_p, v_p, k_s, v_s, delta_p[None], conv_p[None], delta_s[None], conv_s[None])
```

```python
import functools

import jax
import jax.numpy as jnp
from jax import lax
from jax.experimental import pallas as pl
from jax.experimental.pallas import tpu as pltpu

F32, BF16, I32, U32 = jnp.float32, jnp.bfloat16, jnp.int32, jnp.uint32

NORM_EPS = 1e-6
CHUNK = 64
CONV_W = 4
N_EXPERTS = 64
TOP_K = 8
N_GROUPS = 8
TOPK_GROUPS = 4
ROUTED_SCALE = 2.5
HEAD_DIM = 128
GROUP_ROWS = 32
MOE_BM = 256
SB_KEY_TILE = 256
VMEM_LIMIT_BYTES = 56 * 1024 * 1024


def _cparams(*sem):
    return pltpu.CompilerParams(dimension_semantics=sem, vmem_limit_bytes=VMEM_LIMIT_BYTES)


def _sigmoid(x):
    return 1.0 / (1.0 + jnp.exp(-x))


def _silu(x):
    return x * _sigmoid(x)


def _softplus(x):
    return jnp.maximum(x, 0.0) + jnp.log(1.0 + jnp.exp(-jnp.abs(x)))


def _rms(x):
    return x * lax.rsqrt(jnp.mean(x * x, axis=-1, keepdims=True) + NORM_EPS)


def _pack_pair(lo, hi):
    lo_bits = lax.bitcast_convert_type(lo.astype(BF16).astype(F32), U32) >> 16
    hi_bits = lax.bitcast_convert_type(hi.astype(BF16).astype(F32), U32) & jnp.uint32(0xFFFF0000)
    return lo_bits | hi_bits


def _unpack_pair(w):
    lo = lax.bitcast_convert_type(w << 16, F32)
    hi = lax.bitcast_convert_type(w & jnp.uint32(0xFFFF0000), F32)
    return lo, hi


def _linear_kernel(*refs, norm, silu_in, has_mod, has_bias, has_res, prologue, n_sub):
    it = iter(refs)
    x_ref, w_ref = next(it), next(it)
    sh_ref = sc_ref = b_ref = res_ref = gate_ref = xs_ref = None
    if has_mod:
        sh_ref, sc_ref = next(it), next(it)
    if has_bias:
        b_ref = next(it)
    if has_res:
        res_ref, gate_ref = next(it), next(it)
    o_ref = next(it)
    if prologue:
        xs_ref = next(it)

        @pl.when(pl.program_id(1) == 0)
        def _():
            def prep(x):
                x = x.astype(F32)
                if silu_in:
                    x = _silu(x)
                if norm:
                    x = _rms(x)
                return x

            if has_mod:
                for g in range(n_sub):
                    rows = pl.ds(g * GROUP_ROWS, GROUP_ROWS)
                    xg = prep(x_ref[rows, :])
                    xg = xg * (1.0 + sc_ref[g:g + 1, :]) + sh_ref[g:g + 1, :]
                    xs_ref[rows, :] = xg.astype(BF16)
            else:
                xs_ref[...] = prep(x_ref[...]).astype(BF16)

        xs = xs_ref[...]
    else:
        xs = x_ref[...]
    acc = jnp.dot(xs, w_ref[...].astype(BF16), preferred_element_type=F32)
    if has_bias:
        acc = acc + b_ref[...]
    if has_res:
        for g in range(n_sub):
            rows = pl.ds(g * GROUP_ROWS, GROUP_ROWS)
            part = acc[g * GROUP_ROWS:(g + 1) * GROUP_ROWS, :] * gate_ref[g:g + 1, :]
            o_ref[rows, :] = (res_ref[rows, :] + part).astype(o_ref.dtype)
    else:
        o_ref[...] = acc.astype(o_ref.dtype)


def _linear(x, w, *, tm, tn, out_dtype, norm=False, silu_in=False, mod=None, bias=None, res=None, gate=None,
            n_cols=None, col_blk0=0, name="linear"):
    m, k = x.shape
    n = w.shape[1] if n_cols is None else n_cols
    assert m % tm == 0 and n % tn == 0, (m, tm, n, tn)
    has_mod, has_bias, has_res = mod is not None, bias is not None, res is not None
    prologue = norm or silu_in or has_mod or x.dtype != BF16
    n_sub = tm // GROUP_ROWS if (has_mod or has_res) else 0
    in_specs = [pl.BlockSpec((tm, k), lambda i, j: (i, 0)), pl.BlockSpec((k, tn), lambda i, j: (0, col_blk0 + j))]
    args = [x, w]
    if has_mod:
        in_specs += [pl.BlockSpec((n_sub, k), lambda i, j: (i, 0))] * 2
        args += list(mod)
    if has_bias:
        in_specs.append(pl.BlockSpec((1, tn), lambda i, j: (0, j)))
        args.append(bias.reshape(1, n))
    if has_res:
        in_specs += [pl.BlockSpec((tm, tn), lambda i, j: (i, j)), pl.BlockSpec((n_sub, tn), lambda i, j: (i, j))]
        args += [res, gate]
    return pl.pallas_call(
        functools.partial(_linear_kernel, norm=norm, silu_in=silu_in, has_mod=has_mod, has_bias=has_bias,
                          has_res=has_res, prologue=prologue, n_sub=n_sub),
        grid=(m // tm, n // tn),
        in_specs=in_specs,
        out_specs=pl.BlockSpec((tm, tn), lambda i, j: (i, j)),
        out_shape=jax.ShapeDtypeStruct((m, n), out_dtype),
        scratch_shapes=[pltpu.VMEM((tm, k), BF16)] if prologue else [],
        compiler_params=_cparams("parallel", "arbitrary"),
        name=name,
    )(*args)


def _split_bf16(a):
    hi = a.astype(BF16)
    lo = (a - hi.astype(F32)).astype(BF16)
    return hi, lo


def _bmm(a, b):
    return jnp.einsum("bij,bjk->bik", a, b, preferred_element_type=F32)


def _bmm3(a, b):
    ah, al = _split_bf16(a)
    bh, bl = _split_bf16(b)
    m = a.shape[1]
    both = _bmm(jnp.concatenate([ah, al], axis=1), bh)
    return both[:, :m] + both[:, m:] + _bmm(ah, bl)


def _bmm1(a, b):
    return _bmm(a.astype(BF16), b.astype(BF16))


def _unit_lower_inverse(a, c):
    r = lax.broadcasted_iota(I32, (c, c), 0)
    q = lax.broadcasted_iota(I32, (c, c), 1)
    same = lambda s: (r ^ q) < s
    eye = (r == q).astype(F32)
    a0 = jnp.where(same(8), a, 0.0)
    a2 = _bmm1(a0, a0)
    a4 = _bmm1(a2, a2)
    t = eye - a0
    t = t + _bmm1(t, a2)
    t = t + _bmm1(t, a4)
    s = 8
    while s < c:
        a_off = jnp.where(same(2 * s) & jnp.logical_not(same(s)), a, 0.0)
        t = t - _bmm1(_bmm1(t, a_off), t)
        s *= 2
    return t


def _unit_lower_solve(a, rhs, c):
    t = _unit_lower_inverse(a, c).astype(BF16)
    x0 = _bmm(t, rhs.astype(BF16))
    resid = rhs - (x0 + _bmm3(a, x0))
    return x0 + _bmm(t, resid.astype(BF16))


def _gdn_kernel(q_ref, k_ref, v_ref, z_ref, ba_ref, al_ref, dtb_ref, cwq_ref, cwk_ref, cwv_ref,
                c0q_ref, c0k_ref, c0v_ref, s0_ref, on_ref, o_ref, sfin_ref,
                xq_s, xk_s, xv_s, state_s, *, tt_rows, chunk):
    tt = pl.program_id(2)
    n_tt = pl.num_programs(2)
    d = HEAD_DIM
    nc = tt_rows // chunk
    halo = CONV_W - 1

    @pl.when(tt == 0)
    def _():
        state_s[...] = s0_ref[0]
        xq_s[8 - halo:8, :] = c0q_ref[0]
        xk_s[8 - halo:8, :] = c0k_ref[0]
        xv_s[8 - halo:8, :] = c0v_ref[0]

    def conv_silu(x_ref, xs, cw_ref):
        xs[8:8 + tt_rows, :] = x_ref[...]
        acc = xs[8 - halo:8 - halo + tt_rows, :] * cw_ref[0:1, :]
        for i in range(1, CONV_W):
            acc = acc + xs[8 - halo + i:8 - halo + i + tt_rows, :] * cw_ref[i:i + 1, :]
        xs[8 - halo:8, :] = xs[8 + tt_rows - halo:8 + tt_rows, :]
        return _silu(acc)

    qc = conv_silu(q_ref, xq_s, cwq_ref)
    kc = conv_silu(k_ref, xk_s, cwk_ref)
    vc = conv_silu(v_ref, xv_s, cwv_ref)

    def l2(x):
        return x * lax.rsqrt(jnp.sum(x * x, axis=-1, keepdims=True) + NORM_EPS)

    qh = [(l2(qc[:, j * d:(j + 1) * d]) * d ** -0.5).reshape(nc, chunk, d) for j in range(2)]
    kh = [l2(kc[:, j * d:(j + 1) * d]).reshape(nc, chunk, d) for j in range(2)]
    vh = [vc[:, h * d:(h + 1) * d].reshape(nc, chunk, d) for h in range(4)]

    def heads4(per_key_head):
        a, b = per_key_head
        st = jnp.stack([a, a, b, b], axis=1)
        return st.reshape((nc * 4,) + st.shape[2:])

    ba = ba_ref[0]
    beta_all = _sigmoid(ba)
    g_all = -jnp.exp(al_ref[0]) * _softplus(ba + dtb_ref[0])
    mm = lambda a, b: jnp.dot(a, b, preferred_element_type=F32)

    def col4(x, lane0):
        cols = [x[:, lane0 + h:lane0 + h + 1].reshape(nc, chunk, 1) for h in range(4)]
        return jnp.stack(cols, axis=1).reshape(nc * 4, chunk, 1)

    beta = col4(beta_all, 0)
    g_col = col4(g_all, 4)
    nb = nc * 4

    r = lax.broadcasted_iota(I32, (chunk, chunk), 0)
    q_i = lax.broadcasted_iota(I32, (chunk, chunk), 1)
    eye, incl, strict = (r == q_i), (r >= q_i), (r > q_i)
    g_b = jnp.broadcast_to(g_col, (nb, chunk, chunk))
    g_t = jnp.sum(jnp.where(eye, g_b, 0.0), axis=1, keepdims=True)
    gc_t = jnp.sum(jnp.where(r <= q_i, g_b, 0.0), axis=1, keepdims=True)
    gc = jnp.sum(jnp.where(incl, jnp.broadcast_to(g_t, (nb, chunk, chunk)), 0.0), axis=2, keepdims=True)
    decay = jnp.where(incl, jnp.exp(jnp.where(incl, gc - gc_t, 0.0)), 0.0)

    k2 = jnp.concatenate(kh, axis=0).astype(BF16)
    q2 = jnp.concatenate(qh, axis=0).astype(BF16)
    kk = jnp.einsum("bid,bjd->bij", k2, k2, preferred_element_type=F32)
    qk = jnp.einsum("bid,bjd->bij", q2, k2, preferred_element_type=F32)
    kk4 = heads4((kk[:nc], kk[nc:]))
    qk4 = heads4((qk[:nc], qk[nc:]))
    k4 = heads4(kh)
    q4 = heads4(qh)
    v4 = jnp.stack(vh, axis=1).reshape(nb, chunk, d)

    a_mat = jnp.where(strict, kk4 * decay * beta, 0.0)
    attn = (qk4 * decay).astype(BF16)
    eg = jnp.exp(gc)
    g_last = gc[:, chunk - 1:chunk, :]
    rhs = jnp.concatenate([v4 * beta, k4 * (beta * eg)], axis=-1)
    sol = _unit_lower_solve(a_mat, rhs, chunk)
    u_all, w_all = sol[:, :, :d], sol[:, :, d:].astype(BF16)
    qd_all = (q4 * eg).astype(BF16)
    kt_all = k4 * jnp.exp(g_last - gc)
    egl = jnp.exp(g_last)

    on = on_ref[...]
    s = [state_s[h] for h in range(4)]
    for c in range(nc):
        for h in range(4):
            b = c * 4 + h
            sb = s[h].astype(BF16)
            v_new = u_all[b] - mm(w_all[b], sb)
            o = mm(qd_all[b], sb) + mm(attn[b], v_new.astype(BF16))
            s[h] = s[h] * egl[b] + mm(kt_all[b].T.astype(BF16), v_new.astype(BF16))
            rows = pl.ds(c * chunk, chunk)
            z = z_ref[rows, h * d:(h + 1) * d]
            o_ref[rows, h * d:(h + 1) * d] = (_rms(o) * on * _silu(z)).astype(o_ref.dtype)
    for h in range(4):
        state_s[h] = s[h]

    @pl.when(tt == n_tt - 1)
    def _():
        for h in range(4):
            sfin_ref[0, h] = s[h]


def _gated_deltanet(qkvz, ba_g, al_g, dtb_g, conv_w, conv0, s0, o_norm, o_prev, *, batch, seq, row_off, tt_rows,
                    chunk, key_dim, val_dim, name):
    n_tok = qkvz.shape[0]
    d = HEAD_DIM
    n_g = key_dim // (2 * d)
    assert val_dim == 2 * key_dim and seq % tt_rows == 0 and row_off % tt_rows == 0 and tt_rows % chunk == 0
    n_tt = seq // tt_rows
    rb0 = row_off // tt_rows
    kq, kv_ = key_dim // (2 * d), (2 * key_dim) // (4 * d)

    row = lambda b, g, t: rb0 + b * n_tt + t
    in_specs = [
        pl.BlockSpec((tt_rows, 2 * d), lambda b, g, t: (row(b, g, t), g)),
        pl.BlockSpec((tt_rows, 2 * d), lambda b, g, t: (row(b, g, t), kq + g)),
        pl.BlockSpec((tt_rows, 4 * d), lambda b, g, t: (row(b, g, t), kv_ + g)),
        pl.BlockSpec((tt_rows, 4 * d), lambda b, g, t: (row(b, g, t), 2 * kv_ + g)),
        pl.BlockSpec((1, tt_rows, 8), lambda b, g, t: (g, row(b, g, t), 0)),
        pl.BlockSpec((1, 1, 8), lambda b, g, t: (g, 0, 0)),
        pl.BlockSpec((1, 1, 8), lambda b, g, t: (g, 0, 0)),
        pl.BlockSpec((CONV_W, 2 * d), lambda b, g, t: (0, g)),
        pl.BlockSpec((CONV_W, 2 * d), lambda b, g, t: (0, kq + g)),
        pl.BlockSpec((CONV_W, 4 * d), lambda b, g, t: (0, kv_ + g)),
        pl.BlockSpec((1, CONV_W - 1, 2 * d), lambda b, g, t: (b, 0, g)),
        pl.BlockSpec((1, CONV_W - 1, 2 * d), lambda b, g, t: (b, 0, kq + g)),
        pl.BlockSpec((1, CONV_W - 1, 4 * d), lambda b, g, t: (b, 0, kv_ + g)),
        pl.BlockSpec((1, 4, d, d), lambda b, g, t: (b, g, 0, 0)),
        pl.BlockSpec((1, d), lambda b, g, t: (0, 0)),
    ]
    args = [qkvz, qkvz, qkvz, qkvz, ba_g, al_g, dtb_g, conv_w, conv_w, conv_w, conv0, conv0, conv0, s0,
            o_norm.reshape(1, d)]
    aliases = {}
    if o_prev is not None:
        in_specs.append(pl.BlockSpec(memory_space=pl.ANY))
        args.append(o_prev)
        aliases = {len(args) - 1: 0}
    kern = functools.partial(_gdn_kernel, tt_rows=tt_rows, chunk=chunk)
    if o_prev is not None:
        kern = functools.partial(_drop_arg, kern, len(args) - 1)
    return pl.pallas_call(
        kern,
        grid=(batch, n_g, n_tt),
        in_specs=in_specs,
        out_specs=[pl.BlockSpec((tt_rows, 4 * d), lambda b, g, t: (row(b, g, t), g)),
                   pl.BlockSpec((1, 4, d, d), lambda b, g, t: (b, g, 0, 0))],
        out_shape=[jax.ShapeDtypeStruct((n_tok, val_dim), BF16),
                   jax.ShapeDtypeStruct((batch, 4 * n_g, d, d), F32)],
        scratch_shapes=[pltpu.VMEM((8 + tt_rows, 2 * d), F32), pltpu.VMEM((8 + tt_rows, 2 * d), F32),
                        pltpu.VMEM((8 + tt_rows, 4 * d), F32), pltpu.VMEM((4, d, d), F32)],
        input_output_aliases=aliases,
        compiler_params=_cparams("parallel", "parallel", "arbitrary"),
        name=name,
    )(*args)


def _drop_arg(kern, pos, *refs):
    return kern(*refs[:pos], *refs[pos + 1:])


def _sb_kernel(q_ref, k_ref, v_ref, o_ref, kb_s, vb_s, oacc_s, carry_s, *, tq, tk, q_start, grp):
    qi = pl.program_id(2)
    d = HEAD_DIM
    rows = grp * tq

    @pl.when(qi == 0)
    def _():
        kb_s[...] = k_ref[...].astype(BF16)
        vb_s[...] = v_ref[...].astype(BF16)

    q = q_ref[...]
    qs = jnp.concatenate([q[:, h * d:(h + 1) * d] for h in range(grp)], axis=0)
    oacc_s[...] = jnp.zeros_like(oacc_s)
    carry_s[...] = jnp.zeros_like(carry_s)

    q_lo = q_start + qi * tq
    q_hi = q_lo + tq - 1
    n_kt = (q_hi + tk - 1) // tk
    n_full = q_lo // tk

    u = (lax.broadcasted_iota(I32, (tk, tk), 0) > lax.broadcasted_iota(I32, (tk, tk), 1)).astype(BF16)
    u2 = jnp.concatenate([u, u], axis=0)
    qpos = q_lo + (lax.broadcasted_iota(I32, (rows, tk), 0) & (tq - 1))
    lane = lax.broadcasted_iota(I32, (rows, tk), 1)
    c2 = d ** -0.5 * 1.4426950408889634
    sign = jnp.uint32(0x80000000)

    def tile(kt, masked):
        k0 = pl.multiple_of(kt * tk, tk)
        kblk = kb_s[pl.ds(k0, tk), :]
        vblk = vb_s[pl.ds(k0, tk), :]
        z = lax.dot_general(qs, kblk, (((1,), (1,)), ((), ())), preferred_element_type=F32) * c2
        neg_abs = lax.bitcast_convert_type(lax.bitcast_convert_type(z, U32) | sign, F32)
        t = jnp.maximum(z, 0.0) + jnp.log2(1.0 + jnp.exp2(neg_abs))
        if masked:
            m = (k0 + lane) < qpos
            t = jnp.where(m, t, 0.0)
        hi, lo = _split_bf16(t)
        later = jnp.dot(jnp.concatenate([hi, lo], axis=1), u2, preferred_element_type=F32)
        carry = carry_s[...]
        a = jnp.exp2((z - t) - later - jnp.concatenate([carry] * (tk // d), axis=1))
        if masked:
            a = jnp.where(m, a, 0.0)
        oacc_s[...] += jnp.dot(a.astype(BF16), vblk, preferred_element_type=F32)
        carry_s[...] = carry + jnp.broadcast_to(later[:, 0:1] + t[:, 0:1], carry.shape)

    def masked_body(it, c):
        tile(n_kt - 1 - it, True)
        return c

    def full_body(it, c):
        tile(n_full - 1 - it, False)
        return c

    lax.fori_loop(0, n_kt - n_full, masked_body, 0)
    lax.fori_loop(0, n_full, full_body, 0)
    for h in range(grp):
        o_ref[:, h * d:(h + 1) * d] = oacc_s[h * tq:(h + 1) * tq, :].astype(o_ref.dtype)


def _stick_breaking(q, k_arr, v_arr, o_prev, *, batch, seq, q_row_off, kv_len, kv_row_off, k_col0, v_col0,
                    n_kv, grp, q_start, tq, name):
    n_tok, qd = q.shape
    d = HEAD_DIM
    tk = SB_KEY_TILE
    assert seq % tq == 0 and q_row_off % tq == 0 and kv_row_off % kv_len == 0 and kv_len % tk == 0
    assert tq & (tq - 1) == 0
    assert q_start + seq <= kv_len + tk and (q_start + seq - 1 + tk - 1) // tk * tk <= kv_len
    n_q = seq // tq
    in_specs = [
        pl.BlockSpec((tq, grp * d), lambda b, g, i: (q_row_off // tq + b * n_q + i, g)),
        pl.BlockSpec((kv_len, d), lambda b, g, i: (kv_row_off // kv_len + b, k_col0 + g)),
        pl.BlockSpec((kv_len, d), lambda b, g, i: (kv_row_off // kv_len + b, v_col0 + g)),
    ]
    args = [q, k_arr, v_arr]
    kern = functools.partial(_sb_kernel, tq=tq, tk=tk, q_start=q_start, grp=grp)
    aliases = {}
    if o_prev is not None:
        in_specs.append(pl.BlockSpec(memory_space=pl.ANY))
        args.append(o_prev)
        aliases = {3: 0}
        kern = functools.partial(_drop_arg, kern, 3)
    return pl.pallas_call(
        kern,
        grid=(batch, n_kv, n_q),
        in_specs=in_specs,
        out_specs=pl.BlockSpec((tq, grp * d), lambda b, g, i: (q_row_off // tq + b * n_q + i, g)),
        out_shape=jax.ShapeDtypeStruct((n_tok, qd), BF16),
        scratch_shapes=[pltpu.VMEM((kv_len, d), BF16), pltpu.VMEM((kv_len, d), BF16),
                        pltpu.VMEM((grp * tq, d), F32), pltpu.VMEM((grp * tq, d), F32)],
        input_output_aliases=aliases,
        compiler_params=_cparams("parallel", "parallel", "arbitrary"),
        name=name,
    )(*args)


def _router_kernel(x_ref, sh_ref, sc_ref, rwt_ref, rb_ref, hn_ref, idx_ref, wts_ref, hn_s, *, n_sub):
    for g in range(n_sub):
        rows = pl.ds(g * GROUP_ROWS, GROUP_ROWS)
        hn_s[rows, :] = _rms(x_ref[rows, :]) * (1.0 + sc_ref[g:g + 1, :]) + sh_ref[g:g + 1, :]
    hn = hn_s[...]
    half = hn.shape[1] // 2
    hn_ref[...] = _pack_pair(hn[:, :half], hn[:, half:])

    logits = lax.dot_general(rwt_ref[...], hn, (((1,), (1,)), ((), ())), precision=lax.Precision.HIGHEST,
                             preferred_element_type=F32)
    scores = _sigmoid(logits)
    biased = scores + rb_ref[...]
    n_e, tm = biased.shape
    per = n_e // N_GROUPS
    neg = -jnp.inf
    fsum = lambda x: jnp.sum(x, axis=0, keepdims=True)
    fmax = lambda x: jnp.max(x, axis=0, keepdims=True)

    grp_score = []
    for a in range(N_GROUPS):
        blk = biased[a * per:(a + 1) * per]
        m1 = fmax(blk)
        n_top = fsum((blk == m1).astype(F32))
        m2 = fmax(jnp.where(blk < m1, blk, neg))
        grp_score.append(m1 + jnp.where(n_top > 1.5, m1, m2))
    masked = []
    for a in range(N_GROUPS):
        rank = jnp.zeros_like(grp_score[a])
        for b in range(N_GROUPS):
            if b != a:
                ahead = (grp_score[b] > grp_score[a]) | ((grp_score[b] == grp_score[a]) & (b < a))
                rank = rank + ahead.astype(F32)
        masked.append(jnp.where(rank < TOPK_GROUPS - 0.5, biased[a * per:(a + 1) * per], neg))
    mk = jnp.concatenate(masked, axis=0)
    e_id = lax.broadcasted_iota(I32, (n_e, tm), 0).astype(F32)
    ids, ws = [], []
    for _ in range(TOP_K):
        m = fmax(mk)
        first = jnp.min(jnp.where(mk == m, e_id, float(n_e)), axis=0, keepdims=True)
        hit = e_id == first
        ws.append(fsum(jnp.where(hit, scores, 0.0)))
        ids.append(first)
        mk = jnp.where(hit, neg, mk)
    total = ws[0]
    for w in ws[1:]:
        total = total + w
    idx_ref[...] = jnp.concatenate(ids, axis=0).astype(I32)
    wts_ref[...] = jnp.concatenate([w / total * ROUTED_SCALE for w in ws], axis=0)


def _router(x, sh, sc, rw_t, rb, *, tm):
    n_tok, dm = x.shape
    n_e = rw_t.shape[0]
    n_sub = tm // GROUP_ROWS
    return pl.pallas_call(
        functools.partial(_router_kernel, n_sub=n_sub),
        grid=(n_tok // tm,),
        in_specs=[pl.BlockSpec((tm, dm), lambda i: (i, 0)),
                  pl.BlockSpec((n_sub, dm), lambda i: (i, 0)),
                  pl.BlockSpec((n_sub, dm), lambda i: (i, 0)),
                  pl.BlockSpec((n_e, dm), lambda i: (0, 0)),
                  pl.BlockSpec((n_e, 1), lambda i: (0, 0))],
        out_specs=[pl.BlockSpec((tm, dm // 2), lambda i: (i, 0)),
                   pl.BlockSpec((TOP_K, tm), lambda i: (0, i)),
                   pl.BlockSpec((TOP_K, tm), lambda i: (0, i))],
        out_shape=[jax.ShapeDtypeStruct((n_tok, dm // 2), U32),
                   jax.ShapeDtypeStruct((TOP_K, n_tok), I32),
                   jax.ShapeDtypeStruct((TOP_K, n_tok), F32)],
        scratch_shapes=[pltpu.VMEM((tm, dm), F32)],
        compiler_params=_cparams("parallel"),
        name="moe_router",
    )(x, sh, sc, rw_t, rb.reshape(n_e, 1))


IDX_ALIGN = 128
CAST_ROWS = 256
SCATTER_SHIFT = 3
SCATTER_UNROLL = 1 << SCATTER_SHIFT


def _expert_kernel(blk_e_ref, s0_ref, cnt_ref, tok_hbm, slot_hbm, hn_hbm, wgu_ref, wd_ref, *rest, bm, has_prev):
    y_hbm, tbuf, sbuf, xbuf, ybuf, wgu_s, wd_s, gsem, ssem, isem = rest[1:] if has_prev else rest
    i = pl.program_id(0)
    n_blk = pl.num_programs(0)
    cur = i % 2
    win = bm + IDX_ALIGN

    def idx_copies(b):
        a0 = pl.multiple_of(s0_ref[b] & (-IDX_ALIGN), IDX_ALIGN)
        return (pltpu.make_async_copy(tok_hbm.at[pl.ds(a0, win)], tbuf.at[b % 3], isem.at[0, b % 3]),
                pltpu.make_async_copy(slot_hbm.at[pl.ds(a0, win)], sbuf.at[b % 3], isem.at[1, b % 3]))

    def idx_start(b):
        for c in idx_copies(b):
            c.start()

    def idx_wait(b):
        for c in idx_copies(b):
            c.wait()

    def gather(b, buf):
        buf3 = b % 3
        off = s0_ref[b] & (IDX_ALIGN - 1)
        for r in range(bm):
            pltpu.make_async_copy(hn_hbm.at[pl.ds(tbuf[buf3, off + r], 1)], xbuf.at[buf, pl.ds(r, 1)],
                                  gsem.at[buf]).start(priority=r % 2)

    def wait_gather(buf):
        pltpu.make_async_copy(hn_hbm.at[pl.ds(0, bm)], xbuf.at[buf], gsem.at[buf]).wait()

    def scatter_row(buf3, off, r, queue=0):
        pltpu.make_async_copy(ybuf.at[pl.ds(r, 1)], y_hbm.at[pl.ds(sbuf[buf3, off + r], 1)],
                              ssem.at[0]).start(priority=queue)

    def wait_scatter(n_rows):
        n8 = pl.multiple_of((n_rows >> SCATTER_SHIFT) << SCATTER_SHIFT, SCATTER_UNROLL)

        @pl.when(n8 > 0)
        def _():
            pltpu.make_async_copy(ybuf.at[pl.ds(0, n8)], y_hbm.at[pl.ds(0, n8)], ssem.at[0]).wait()

        def one_row(r, c):
            pltpu.make_async_copy(ybuf.at[pl.ds(0, 1)], y_hbm.at[pl.ds(0, 1)], ssem.at[0]).wait()
            return c

        lax.fori_loop(n8, n_rows, one_row, 0)

    @pl.when(i == 0)
    def _():
        idx_start(0)
        idx_wait(0)
        gather(0, 0)

        @pl.when(n_blk > 1)
        def _():
            idx_start(1)

    @pl.when(i + 1 < n_blk)
    def _():
        idx_wait(i + 1)
        gather(i + 1, 1 - cur)

    @pl.when(i + 2 < n_blk)
    def _():
        idx_start(i + 2)

    @pl.when((i == 0) | (blk_e_ref[i] != blk_e_ref[jnp.maximum(i - 1, 0)]))
    def _():
        for dst, src in ((wgu_s, wgu_ref), (wd_s, wd_ref)):
            step = min(CAST_ROWS, dst.shape[0])
            for r0 in range(0, dst.shape[0], step):
                dst[r0:r0 + step, :] = src[0, r0:r0 + step, :].astype(BF16)

    wait_gather(cur)
    lo, hi = _unpack_pair(xbuf[cur])
    half = lo.shape[1]
    gu = (jnp.dot(lo.astype(BF16), wgu_s[:half, :], preferred_element_type=F32)
          + jnp.dot(hi.astype(BF16), wgu_s[half:, :], preferred_element_type=F32))
    de = gu.shape[1] // 2
    h = (_silu(gu[:, :de]) * gu[:, de:]).astype(BF16)
    y = jnp.dot(h, wd_s[...], preferred_element_type=F32)

    @pl.when(i > 0)
    def _():
        wait_scatter(cnt_ref[jnp.maximum(i - 1, 0)])

    ybuf[...] = _pack_pair(y[:, :half], y[:, half:])
    cnt = cnt_ref[i]
    buf3 = i % 3
    off = s0_ref[i] & (IDX_ALIGN - 1)
    n_grp = cnt >> SCATTER_SHIFT

    def grp_body(j, c):
        for u in range(SCATTER_UNROLL):
            scatter_row(buf3, off, j * SCATTER_UNROLL + u, queue=u % 2)
        return c

    def row_body(r, c):
        scatter_row(buf3, off, r)
        return c

    lax.fori_loop(0, n_grp, grp_body, 0)
    lax.fori_loop(n_grp * SCATTER_UNROLL, cnt, row_body, 0)

    @pl.when(i == n_blk - 1)
    def _():
        wait_scatter(cnt)


def _experts(hn_packed, wgu, wd, plan, y_prev, *, n_slots, bm, name):
    blk_e, s0, cnt, tok, slot = plan
    n_blk = blk_e.shape[0]
    n_tok, half = hn_packed.shape
    _, dm, de2 = wgu.shape
    any_spec = pl.BlockSpec(memory_space=pl.ANY)
    in_specs = [any_spec, any_spec, any_spec,
                pl.BlockSpec((1, dm, de2), lambda i, e, s, c: (e[i], 0, 0)),
                pl.BlockSpec((1, de2 // 2, dm), lambda i, e, s, c: (e[i], 0, 0))]
    args = [blk_e, s0, cnt, tok, slot, hn_packed, wgu, wd]
    aliases = {}
    if y_prev is not None:
        in_specs.append(any_spec)
        args.append(y_prev)
        aliases = {len(args) - 1: 0}
    grid_spec = pltpu.PrefetchScalarGridSpec(
        num_scalar_prefetch=3,
        grid=(n_blk,),
        in_specs=in_specs,
        out_specs=any_spec,
        scratch_shapes=[pltpu.SMEM((3, bm + IDX_ALIGN), I32), pltpu.SMEM((3, bm + IDX_ALIGN), I32),
                        pltpu.VMEM((2, bm, half), U32), pltpu.VMEM((bm, half), U32),
                        pltpu.VMEM((dm, de2), BF16), pltpu.VMEM((de2 // 2, dm), BF16),
                        pltpu.SemaphoreType.DMA((2,)), pltpu.SemaphoreType.DMA((1,)), pltpu.SemaphoreType.DMA((2, 3))],
    )
    return pl.pallas_call(
        functools.partial(_expert_kernel, bm=bm, has_prev=y_prev is not None),
        grid_spec=grid_spec,
        out_shape=jax.ShapeDtypeStruct((n_slots, half), U32),
        input_output_aliases=aliases,
        compiler_params=_cparams("arbitrary"),
        name=name,
    )(*args)


def _combine_kernel(x_ref, y_ref, w_ref, gate_ref, fn_ref, *rest, n_sub, final, n_first):
    outs, ff_s = rest[:-1], rest[-1]
    n_k = y_ref.shape[0]
    half = y_ref.shape[2]
    w = w_ref[...]
    lo, hi = _unpack_pair(y_ref[n_k - 1])
    for k in range(n_k - 1):
        yl, yh = _unpack_pair(y_ref[k])
        wk = w[:, k:k + 1]
        lo = lo + wk * yl
        hi = hi + wk * yh
    ff_s[:, :half] = lo
    ff_s[:, half:] = hi
    for g in range(n_sub):
        rows = pl.ds(g * GROUP_ROWS, GROUP_ROWS)
        out = x_ref[rows, :] + gate_ref[g:g + 1, :] * ff_s[rows, :]
        if final:
            out = _rms(out) * fn_ref[...]
        ff_s[rows, :] = out
    if n_first is None:
        outs[0][...] = ff_s[...]
    else:
        i = pl.program_id(0)

        @pl.when(i < n_first)
        def _():
            outs[0][...] = ff_s[...]

        @pl.when(i >= n_first)
        def _():
            outs[1][...] = ff_s[...]


def _combine(x, y_slots, wts_tok, gate, final_norm, *, tm, final, n_first_rows=None):
    n_tok, dm = x.shape
    n_k = TOP_K + 1
    n_sub = tm // GROUP_ROWS
    y3 = y_slots.reshape(n_k, n_tok, dm // 2)
    if n_first_rows is None:
        n_first = None
        out_specs = pl.BlockSpec((tm, dm), lambda i: (i, 0))
        out_shape = jax.ShapeDtypeStruct((n_tok, dm), F32)
    else:
        assert n_first_rows % tm == 0 and n_tok - n_first_rows == tm
        n_first = n_first_rows // tm
        out_specs = [pl.BlockSpec((tm, dm), lambda i: (jnp.minimum(i, n_first - 1), 0)),
                     pl.BlockSpec((tm, dm), lambda i: (0, 0))]
        out_shape = [jax.ShapeDtypeStruct((n_first_rows, dm), F32), jax.ShapeDtypeStruct((tm, dm), F32)]
    return pl.pallas_call(
        functools.partial(_combine_kernel, n_sub=n_sub, final=final, n_first=n_first),
        grid=(n_tok // tm,),
        in_specs=[pl.BlockSpec((tm, dm), lambda i: (i, 0)),
                  pl.BlockSpec((n_k, tm, dm // 2), lambda i: (0, i, 0)),
                  pl.BlockSpec((tm, TOP_K), lambda i: (i, 0)),
                  pl.BlockSpec((n_sub, dm), lambda i: (i, 0)),
                  pl.BlockSpec((1, dm), lambda i: (0, 0))],
        out_specs=out_specs,
        out_shape=out_shape,
        scratch_shapes=[pltpu.VMEM((tm, dm), F32)],
        compiler_params=_cparams("arbitrary"),
        name="moe_combine",
    )(x, y3, wts_tok, gate, final_norm.reshape(1, dm))


def _moe_plan(idx_t, bm):
    n_tok = idx_t.shape[1]
    assert n_tok < (1 << 16)
    flat_e = idx_t.reshape(-1)
    a = flat_e.shape[0]
    _, order = lax.sort((flat_e, jnp.arange(a, dtype=I32)), num_keys=1, is_stable=True)
    tail = jnp.zeros((bm + 2 * IDX_ALIGN,), I32)
    tok = jnp.concatenate([order % n_tok, tail])
    slot = jnp.concatenate([order, tail])
    e_ids = jnp.arange(N_EXPERTS, dtype=I32)
    counts = jnp.sum((flat_e[:, None] == e_ids[None, :]).astype(I32), axis=0)
    start = jnp.cumsum(counts) - counts
    nblk_e = (counts + bm - 1) // bm
    blk_end = jnp.cumsum(nblk_e)
    n_blk = -(-(a + N_EXPERTS * (bm - 1)) // bm)
    b = jnp.arange(n_blk, dtype=I32)
    blk_e = jnp.minimum(jnp.sum((b[:, None] >= blk_end[None, :]).astype(I32), axis=1), N_EXPERTS - 1)
    onehot = (blk_e[:, None] == e_ids[None, :]).astype(I32)
    pick = lambda v: jnp.sum(onehot * v[None, :], axis=1)
    j = b - pick(blk_end - nblk_e)
    s0 = jnp.minimum(pick(start) + j * bm, a)
    cnt = jnp.clip(pick(counts) - j * bm, 0, bm)
    return blk_e, s0, cnt, tok, slot


def _shared_plan(n_tok, bm):
    assert n_tok % bm == 0
    n_blk = n_tok // bm
    tail = jnp.zeros((bm + 2 * IDX_ALIGN,), I32)
    tok = jnp.concatenate([jnp.arange(n_tok, dtype=I32), tail])
    slot = jnp.concatenate([TOP_K * n_tok + jnp.arange(n_tok, dtype=I32), tail])
    return (jnp.zeros((n_blk,), I32), jnp.arange(n_blk, dtype=I32) * bm, jnp.full((n_blk,), bm, I32), tok, slot)


def _moe(x, sh, sc, gate, router_w, router_bias, w_gate_up, w_down, ws_gate_up, ws_down, final_norm, *, final, tm,
         n_first_rows=None):
    n_tok, dm = x.shape
    hn_packed, idx_t, wts_t = _router(x, sh, sc, router_w.T, router_bias, tm=tm)
    n_slots = (TOP_K + 1) * n_tok
    y_slots = _experts(hn_packed, w_gate_up, w_down, _moe_plan(idx_t, MOE_BM), None, n_slots=n_slots,
                       bm=MOE_BM, name="moe_experts")
    y_slots = _experts(hn_packed, ws_gate_up[None], ws_down[None], _shared_plan(n_tok, MOE_BM), y_slots,
                       n_slots=n_slots, bm=MOE_BM, name="moe_shared_expert")
    return _combine(x, y_slots, wts_t.T, gate, final_norm, tm=tm, final=final, n_first_rows=n_first_rows)


def _group_rows(mod, bp, tp, bs, ts):
    return jnp.concatenate([jnp.repeat(mod[:bp], tp // GROUP_ROWS, axis=0),
                            jnp.repeat(mod[bp:bp + bs], ts // GROUP_ROWS, axis=0)], axis=0)


def kernel(x_prompt, x_sample, c_prompt, c_sample, state_delta, state_conv, cache_k, cache_v, ada_w, ada_b, a_w_in, a_conv_w, a_a_log, a_dt_bias, a_o_norm, a_w_out, kv_ada_w, kv_ada_b, w_kv, b_w_q, b_w_out, router_w, router_bias, w_gate_up, w_down, ws_gate_up, ws_down, final_norm):
    bp, tp, dm = x_prompt.shape
    bs, ts, _ = x_sample.shape
    n_a = a_w_in.shape[0]
    depth = ada_w.shape[0]
    assert n_a == 1 and depth == 2, "one DeltaNet layer followed by one attention layer"
    assert ts == GROUP_ROWS and tp % GROUP_ROWS == 0
    n_p, n_s = bp * tp, bs * ts
    n_tok = n_p + n_s
    d = HEAD_DIM
    hv = state_delta.shape[2]
    val_dim = hv * d
    key_dim = val_dim // 2
    conv_dim = 2 * key_dim + val_dim
    past = cache_k.shape[1]
    n_kv = cache_k.shape[2]
    grp = b_w_q.shape[2] // (n_kv * d)
    tm_big = 1280 if n_tok % 1280 == 0 else 256
    tm_tok = 256

    x = jnp.concatenate([x_prompt.reshape(n_p, dm), x_sample.reshape(n_s, dm)], axis=0)
    c_rows = 16
    c_all = jnp.concatenate([c_prompt, c_sample, jnp.zeros((c_rows - bp - bs, dm), F32)], axis=0)
    groups = lambda m: _group_rows(m, bp, tp, bs, ts)

    def ada(w, b):
        n = w.shape[1]
        return _linear(c_all, w, tm=c_rows, tn=1024, out_dtype=F32, silu_in=True, bias=b, name="adaln")

    mod = ada(ada_w[0], ada_b[0])
    sh_m, sc_m, gt_m, sh_f, sc_f, gt_f = [groups(m) for m in jnp.split(mod, 6, axis=-1)]
    w_in = a_w_in[0].astype(BF16)
    n_main = conv_dim + val_dim
    qkvz = _linear(x, w_in, tm=tm_big, tn=1024, out_dtype=F32, norm=True, mod=(sh_m, sc_m), n_cols=n_main,
                   name="gdn_in_proj")
    gate_tn = 128
    assert n_main % gate_tn == 0 and 2 * hv <= gate_tn
    ba = _linear(x, w_in, tm=tm_big, tn=gate_tn, out_dtype=F32, norm=True, mod=(sh_m, sc_m), n_cols=gate_tn,
                 col_blk0=n_main // gate_tn, name="gdn_gate_proj")[:, :2 * hv]
    n_g = hv // 4
    ba_g = ba.reshape(n_tok, 2, n_g, 4).transpose(2, 0, 1, 3).reshape(n_g, n_tok, 8)
    pad4 = jnp.zeros((n_g, 4), F32)
    al_g = jnp.concatenate([pad4, a_a_log[0].reshape(n_g, 4)], axis=1).reshape(n_g, 1, 8)
    dtb_g = jnp.concatenate([pad4, a_dt_bias[0].reshape(n_g, 4)], axis=1).reshape(n_g, 1, 8)
    gdn = functools.partial(_gated_deltanet, qkvz, ba_g, al_g, dtb_g, a_conv_w[0], key_dim=key_dim, val_dim=val_dim)
    o_gdn, delta_p = gdn(jnp.zeros((bp, CONV_W - 1, conv_dim), F32), jnp.zeros((bp, hv, d, d), F32), a_o_norm[0], None,
                         batch=bp, seq=tp, row_off=0, tt_rows=512, chunk=CHUNK, name="gdn_prompt")
    o_gdn, delta_s = gdn(state_conv[0], state_delta[0], a_o_norm[0], o_gdn,
                         batch=bs, seq=ts, row_off=n_p, tt_rows=ts, chunk=min(CHUNK, ts), name="gdn_sample")
    last_rows = lambda r0, t: lax.slice(qkvz, (r0 + t - (CONV_W - 1), 0), (r0 + t, conv_dim))
    conv_p = jnp.stack([last_rows(b * tp, tp) for b in range(bp)])
    conv_s = jnp.stack([last_rows(n_p + b * ts, ts) for b in range(bs)])
    x = _linear(o_gdn, a_w_out[0].astype(BF16), tm=tm_big, tn=512, out_dtype=F32, res=x, gate=gt_m, name="gdn_out_proj")
    x = _moe(x, sh_f, sc_f, gt_f, router_w[0], router_bias[0], w_gate_up[0], w_down[0], ws_gate_up[0], ws_down[0],
             final_norm, final=False, tm=tm_tok)

    mod = ada(ada_w[1], ada_b[1])
    sh_m, sc_m, gt_m, sh_f, sc_f, gt_f = [groups(m) for m in jnp.split(mod, 6, axis=-1)]
    kv_mod = ada(kv_ada_w, kv_ada_b)
    kv_sh, kv_sc = [groups(m) for m in jnp.split(kv_mod, 2, axis=-1)]
    kv = _linear(x, w_kv.astype(BF16), tm=tm_big, tn=1024, out_dtype=F32, norm=True, mod=(kv_sh, kv_sc), name="kv_proj")
    q = _linear(x, b_w_q[0].astype(BF16), tm=tm_big, tn=1024, out_dtype=BF16, norm=True, mod=(sh_m, sc_m), name="q_proj")
    kd = n_kv * d
    k_p = kv[:n_p, :kd].reshape(bp, tp, n_kv, d)
    v_p = kv[:n_p, kd:].reshape(bp, tp, n_kv, d)
    k_s = kv[n_p:, :kd].reshape(bs, ts, n_kv, d)
    v_s = kv[n_p:, kd:].reshape(bs, ts, n_kv, d)
    kv_len_s = -(-(past + ts) // SB_KEY_TILE) * SB_KEY_TILE
    tail = jnp.zeros((bs, kv_len_s - past - ts, kd), F32)
    k_all = jnp.concatenate([cache_k.reshape(bs, past, kd), k_s.reshape(bs, ts, kd), tail], axis=1).reshape(-1, kd)
    v_all = jnp.concatenate([cache_v.reshape(bs, past, kd), v_s.reshape(bs, ts, kd), tail], axis=1).reshape(-1, kd)
    o_att = _stick_breaking(q, kv, kv, None, batch=bp, seq=tp, q_row_off=0, kv_len=tp, kv_row_off=0, k_col0=0,
                            v_col0=n_kv, n_kv=n_kv, grp=grp, q_start=0, tq=256, name="attn_prompt")
    o_att = _stick_breaking(q, k_all, v_all, o_att, batch=bs, seq=ts, q_row_off=n_p, kv_len=kv_len_s, kv_row_off=0,
                            k_col0=0, v_col0=0, n_kv=n_kv, grp=grp, q_start=past, tq=ts, name="attn_sample")
    x = _linear(o_att, b_w_out[0].astype(BF16), tm=tm_big, tn=1024, out_dtype=F32, res=x, gate=gt_m, name="attn_out_proj")
    assert n_s == tm_tok
    y_p, y_s = _moe(x, sh_f, sc_f, gt_f, router_w[1], router_bias[1], w_gate_up[1], w_down[1], ws_gate_up[1],
                    ws_down[1], final_norm, final=True, tm=tm_tok, n_first_rows=n_p)
    y_p = y_p.reshape(bp, tp, dm)
    y_s = y_s.reshape(bs, ts, dm)
    return (y_p, y_s, k_p, v_p, k_s, v_s, delta_p[None], conv_p[None], delta_s[None], conv_s[None])
```

```python
import functools

import jax
import jax.numpy as jnp
from jax import lax
from jax.experimental import pallas as pl
from jax.experimental.pallas import tpu as pltpu

F32, BF16, I32, U32 = jnp.float32, jnp.bfloat16, jnp.int32, jnp.uint32

NORM_EPS = 1e-6
CHUNK = 64
CONV_W = 4
N_EXPERTS = 64
TOP_K = 8
N_GROUPS = 8
TOPK_GROUPS = 4
ROUTED_SCALE = 2.5
HEAD_DIM = 128
LANES = 128
GROUP_ROWS = 32
MOE_BM = 256
SB_KEY_TILE = 256
VMEM_LIMIT_BYTES = 56 * 1024 * 1024


def _cparams(*sem):
    return pltpu.CompilerParams(dimension_semantics=sem, vmem_limit_bytes=VMEM_LIMIT_BYTES)


def _sigmoid(x):
    return 1.0 / (1.0 + jnp.exp(-x))


def _silu(x):
    return x * _sigmoid(x)


def _softplus(x):
    return jnp.maximum(x, 0.0) + jnp.log(1.0 + jnp.exp(-jnp.abs(x)))


def _rms(x):
    return x * lax.rsqrt(jnp.mean(x * x, axis=-1, keepdims=True) + NORM_EPS)


def _pack_pair(lo, hi):
    lo_bits = lax.bitcast_convert_type(lo.astype(BF16).astype(F32), U32) >> 16
    hi_bits = lax.bitcast_convert_type(hi.astype(BF16).astype(F32), U32) & jnp.uint32(0xFFFF0000)
    return lo_bits | hi_bits


def _unpack_pair(w):
    lo = lax.bitcast_convert_type(w << 16, F32)
    hi = lax.bitcast_convert_type(w & jnp.uint32(0xFFFF0000), F32)
    return lo, hi


def _store_token_tiles(ref, x, base=None):
    n, dm = x.shape
    half = dm // 2
    spt = half // LANES
    words = _pack_pair(x[:, :half], x[:, half:])
    for s in range(spt):
        piece = words[:, s * LANES:(s + 1) * LANES]
        if base is None:
            ref[pl.ds(s, n, stride=spt), :] = piece
        else:
            ref[base, pl.ds(s, n, stride=spt), :] = piece


def _load_token_tiles(ref, n, spt, base=None):
    los, his = [], []
    for s in range(spt):
        w = ref[pl.ds(s, n, stride=spt), :] if base is None else ref[base, pl.ds(s, n, stride=spt), :]
        lo, hi = _unpack_pair(w)
        los.append(lo)
        his.append(hi)
    return los, his


def _linear_kernel(*refs, norm, silu_in, has_mod, has_bias, has_res, prologue, n_sub):
    it = iter(refs)
    x_ref, w_ref = next(it), next(it)
    sh_ref = sc_ref = b_ref = res_ref = gate_ref = xs_ref = None
    if has_mod:
        sh_ref, sc_ref = next(it), next(it)
    if has_bias:
        b_ref = next(it)
    if has_res:
        res_ref, gate_ref = next(it), next(it)
    o_ref = next(it)
    if prologue:
        xs_ref = next(it)

        @pl.when(pl.program_id(1) == 0)
        def _():
            def prep(x):
                x = x.astype(F32)
                if silu_in:
                    x = _silu(x)
                if norm:
                    x = _rms(x)
                return x

            if has_mod:
                for g in range(n_sub):
                    rows = pl.ds(g * GROUP_ROWS, GROUP_ROWS)
                    xg = prep(x_ref[rows, :])
                    xg = xg * (1.0 + sc_ref[g:g + 1, :]) + sh_ref[g:g + 1, :]
                    xs_ref[rows, :] = xg.astype(BF16)
            else:
                xs_ref[...] = prep(x_ref[...]).astype(BF16)

        xs = xs_ref[...]
    else:
        xs = x_ref[...]
    acc = jnp.dot(xs, w_ref[...].astype(BF16), preferred_element_type=F32)
    if has_bias:
        acc = acc + b_ref[...]
    if has_res:
        for g in range(n_sub):
            rows = pl.ds(g * GROUP_ROWS, GROUP_ROWS)
            part = acc[g * GROUP_ROWS:(g + 1) * GROUP_ROWS, :] * gate_ref[g:g + 1, :]
            o_ref[rows, :] = (res_ref[rows, :] + part).astype(o_ref.dtype)
    else:
        o_ref[...] = acc.astype(o_ref.dtype)


def _linear(x, w, *, tm, tn, out_dtype, norm=False, silu_in=False, mod=None, bias=None, res=None, gate=None,
            n_cols=None, col_blk0=0, layer=None, name="linear"):
    m, k = x.shape
    n = w.shape[-1] if n_cols is None else n_cols
    assert m % tm == 0 and n % tn == 0, (m, tm, n, tn)
    has_mod, has_bias, has_res = mod is not None, bias is not None, res is not None
    prologue = norm or silu_in or has_mod or x.dtype != BF16
    n_sub = tm // GROUP_ROWS if (has_mod or has_res) else 0
    if layer is None:
        w_spec = pl.BlockSpec((k, tn), lambda i, j: (0, col_blk0 + j))
    else:
        w_spec = pl.BlockSpec((None, k, tn), lambda i, j: (layer, 0, col_blk0 + j))
    in_specs = [pl.BlockSpec((tm, k), lambda i, j: (i, 0)), w_spec]
    args = [x, w]
    if has_mod:
        in_specs += [pl.BlockSpec((n_sub, k), lambda i, j: (i, 0))] * 2
        args += list(mod)
    if has_bias:
        in_specs.append(pl.BlockSpec((1, tn), lambda i, j: (0, j)))
        args.append(bias.reshape(1, n))
    if has_res:
        in_specs += [pl.BlockSpec((tm, tn), lambda i, j: (i, j)), pl.BlockSpec((n_sub, tn), lambda i, j: (i, j))]
        args += [res, gate]
    return pl.pallas_call(
        functools.partial(_linear_kernel, norm=norm, silu_in=silu_in, has_mod=has_mod, has_bias=has_bias,
                          has_res=has_res, prologue=prologue, n_sub=n_sub),
        grid=(m // tm, n // tn),
        in_specs=in_specs,
        out_specs=pl.BlockSpec((tm, tn), lambda i, j: (i, j)),
        out_shape=jax.ShapeDtypeStruct((m, n), out_dtype),
        scratch_shapes=[pltpu.VMEM((tm, k), BF16)] if prologue else [],
        compiler_params=_cparams("parallel", "arbitrary"),
        name=name,
    )(*args)


def _split_bf16(a):
    hi = a.astype(BF16)
    lo = (a - hi.astype(F32)).astype(BF16)
    return hi, lo


def _bmm(a, b):
    return jnp.einsum("bij,bjk->bik", a, b, preferred_element_type=F32)


def _bmm3(a, b):
    ah, al = _split_bf16(a)
    bh, bl = _split_bf16(b)
    m = a.shape[1]
    both = _bmm(jnp.concatenate([ah, al], axis=1), bh)
    return both[:, :m] + both[:, m:] + _bmm(ah, bl)


def _bmm1(a, b):
    return _bmm(a.astype(BF16), b.astype(BF16))


def _unit_lower_inverse(a, c):
    r = lax.broadcasted_iota(I32, (c, c), 0)
    q = lax.broadcasted_iota(I32, (c, c), 1)
    same = lambda s: (r ^ q) < s
    eye = (r == q).astype(F32)
    a0 = jnp.where(same(8), a, 0.0)
    a2 = _bmm1(a0, a0)
    a4 = _bmm1(a2, a2)
    t = eye - a0
    t = t + _bmm1(t, a2)
    t = t + _bmm1(t, a4)
    s = 8
    while s < c:
        a_off = jnp.where(same(2 * s) & jnp.logical_not(same(s)), a, 0.0)
        t = t - _bmm1(_bmm1(t, a_off), t)
        s *= 2
    return t


def _unit_lower_solve(a, rhs, c):
    t = _unit_lower_inverse(a, c).astype(BF16)
    x0 = _bmm(t, rhs.astype(BF16))
    resid = rhs - (x0 + _bmm3(a, x0))
    return x0 + _bmm(t, resid.astype(BF16))


def _gdn_kernel(q_ref, k_ref, v_ref, z_ref, ba_ref, al_ref, dtb_ref, cwq_ref, cwk_ref, cwv_ref,
                c0q_ref, c0k_ref, c0v_ref, s0_ref, on_ref, o_ref, sfin_ref,
                xq_s, xk_s, xv_s, state_s, *, tt_rows, chunk):
    tt = pl.program_id(2)
    n_tt = pl.num_programs(2)
    d = HEAD_DIM
    nc = tt_rows // chunk
    halo = CONV_W - 1

    @pl.when(tt == 0)
    def _():
        state_s[...] = s0_ref[0]
        xq_s[8 - halo:8, :] = c0q_ref[0]
        xk_s[8 - halo:8, :] = c0k_ref[0]
        xv_s[8 - halo:8, :] = c0v_ref[0]

    def conv_silu(x_ref, xs, cw_ref):
        xs[8:8 + tt_rows, :] = x_ref[...]
        acc = xs[8 - halo:8 - halo + tt_rows, :] * cw_ref[0:1, :]
        for i in range(1, CONV_W):
            acc = acc + xs[8 - halo + i:8 - halo + i + tt_rows, :] * cw_ref[i:i + 1, :]
        xs[8 - halo:8, :] = xs[8 + tt_rows - halo:8 + tt_rows, :]
        return _silu(acc)

    qc = conv_silu(q_ref, xq_s, cwq_ref)
    kc = conv_silu(k_ref, xk_s, cwk_ref)
    vc = conv_silu(v_ref, xv_s, cwv_ref)

    def l2(x):
        return x * lax.rsqrt(jnp.sum(x * x, axis=-1, keepdims=True) + NORM_EPS)

    qh = [(l2(qc[:, j * d:(j + 1) * d]) * d ** -0.5).reshape(nc, chunk, d) for j in range(2)]
    kh = [l2(kc[:, j * d:(j + 1) * d]).reshape(nc, chunk, d) for j in range(2)]
    vh = [vc[:, h * d:(h + 1) * d].reshape(nc, chunk, d) for h in range(4)]

    def heads4(per_key_head):
        a, b = per_key_head
        st = jnp.stack([a, a, b, b], axis=1)
        return st.reshape((nc * 4,) + st.shape[2:])

    ba = ba_ref[0]
    beta_all = _sigmoid(ba)
    g_all = -jnp.exp(al_ref[0]) * _softplus(ba + dtb_ref[0])
    mm = lambda a, b: jnp.dot(a, b, preferred_element_type=F32)

    def col4(x, lane0):
        cols = [x[:, lane0 + h:lane0 + h + 1].reshape(nc, chunk, 1) for h in range(4)]
        return jnp.stack(cols, axis=1).reshape(nc * 4, chunk, 1)

    beta = col4(beta_all, 0)
    g_col = col4(g_all, 4)
    nb = nc * 4

    r = lax.broadcasted_iota(I32, (chunk, chunk), 0)
    q_i = lax.broadcasted_iota(I32, (chunk, chunk), 1)
    eye, incl, strict = (r == q_i), (r >= q_i), (r > q_i)
    g_b = jnp.broadcast_to(g_col, (nb, chunk, chunk))
    g_t = jnp.sum(jnp.where(eye, g_b, 0.0), axis=1, keepdims=True)
    gc_t = jnp.sum(jnp.where(r <= q_i, g_b, 0.0), axis=1, keepdims=True)
    gc = jnp.sum(jnp.where(incl, jnp.broadcast_to(g_t, (nb, chunk, chunk)), 0.0), axis=2, keepdims=True)
    decay = jnp.where(incl, jnp.exp(jnp.where(incl, gc - gc_t, 0.0)), 0.0)

    k2 = jnp.concatenate(kh, axis=0).astype(BF16)
    q2 = jnp.concatenate(qh, axis=0).astype(BF16)
    kk = jnp.einsum("bid,bjd->bij", k2, k2, preferred_element_type=F32)
    qk = jnp.einsum("bid,bjd->bij", q2, k2, preferred_element_type=F32)
    kk4 = heads4((kk[:nc], kk[nc:]))
    qk4 = heads4((qk[:nc], qk[nc:]))
    k4 = heads4(kh)
    q4 = heads4(qh)
    v4 = jnp.stack(vh, axis=1).reshape(nb, chunk, d)

    a_mat = jnp.where(strict, kk4 * decay * beta, 0.0)
    attn = (qk4 * decay).astype(BF16)
    eg = jnp.exp(gc)
    g_last = gc[:, chunk - 1:chunk, :]
    rhs = jnp.concatenate([v4 * beta, k4 * (beta * eg)], axis=-1)
    sol = _unit_lower_solve(a_mat, rhs, chunk)
    u_all, w_all = sol[:, :, :d], sol[:, :, d:].astype(BF16)
    qd_all = (q4 * eg).astype(BF16)
    kt_all = k4 * jnp.exp(g_last - gc)
    egl = jnp.exp(g_last)

    on = on_ref[...]
    s = [state_s[h] for h in range(4)]
    for c in range(nc):
        for h in range(4):
            b = c * 4 + h
            sb = s[h].astype(BF16)
            v_new = u_all[b] - mm(w_all[b], sb)
            o = mm(qd_all[b], sb) + mm(attn[b], v_new.astype(BF16))
            s[h] = s[h] * egl[b] + mm(kt_all[b].T.astype(BF16), v_new.astype(BF16))
            rows = pl.ds(c * chunk, chunk)
            z = z_ref[rows, h * d:(h + 1) * d]
            o_ref[rows, h * d:(h + 1) * d] = (_rms(o) * on * _silu(z)).astype(o_ref.dtype)
    for h in range(4):
        state_s[h] = s[h]

    @pl.when(tt == n_tt - 1)
    def _():
        for h in range(4):
            sfin_ref[0, h] = s[h]


def _gated_deltanet(qkvz, ba_g, al_g, dtb_g, conv_w, conv0, s0, o_norm, o_prev, *, batch, seq, row_off, tt_rows,
                    chunk, key_dim, val_dim, name):
    n_tok = qkvz.shape[0]
    d = HEAD_DIM
    n_g = key_dim // (2 * d)
    assert val_dim == 2 * key_dim and seq % tt_rows == 0 and row_off % tt_rows == 0 and tt_rows % chunk == 0
    n_tt = seq // tt_rows
    rb0 = row_off // tt_rows
    kq, kv_ = key_dim // (2 * d), (2 * key_dim) // (4 * d)

    row = lambda b, g, t: rb0 + b * n_tt + t
    in_specs = [
        pl.BlockSpec((tt_rows, 2 * d), lambda b, g, t: (row(b, g, t), g)),
        pl.BlockSpec((tt_rows, 2 * d), lambda b, g, t: (row(b, g, t), kq + g)),
        pl.BlockSpec((tt_rows, 4 * d), lambda b, g, t: (row(b, g, t), kv_ + g)),
        pl.BlockSpec((tt_rows, 4 * d), lambda b, g, t: (row(b, g, t), 2 * kv_ + g)),
        pl.BlockSpec((1, tt_rows, 8), lambda b, g, t: (g, row(b, g, t), 0)),
        pl.BlockSpec((1, 1, 8), lambda b, g, t: (g, 0, 0)),
        pl.BlockSpec((1, 1, 8), lambda b, g, t: (g, 0, 0)),
        pl.BlockSpec((CONV_W, 2 * d), lambda b, g, t: (0, g)),
        pl.BlockSpec((CONV_W, 2 * d), lambda b, g, t: (0, kq + g)),
        pl.BlockSpec((CONV_W, 4 * d), lambda b, g, t: (0, kv_ + g)),
        pl.BlockSpec((1, CONV_W - 1, 2 * d), lambda b, g, t: (b, 0, g)),
        pl.BlockSpec((1, CONV_W - 1, 2 * d), lambda b, g, t: (b, 0, kq + g)),
        pl.BlockSpec((1, CONV_W - 1, 4 * d), lambda b, g, t: (b, 0, kv_ + g)),
        pl.BlockSpec((1, 4, d, d), lambda b, g, t: (b, g, 0, 0)),
        pl.BlockSpec((1, d), lambda b, g, t: (0, 0)),
    ]
    args = [qkvz, qkvz, qkvz, qkvz, ba_g, al_g, dtb_g, conv_w, conv_w, conv_w, conv0, conv0, conv0, s0,
            o_norm.reshape(1, d)]
    aliases = {}
    if o_prev is not None:
        in_specs.append(pl.BlockSpec(memory_space=pl.ANY))
        args.append(o_prev)
        aliases = {len(args) - 1: 0}
    kern = functools.partial(_gdn_kernel, tt_rows=tt_rows, chunk=chunk)
    if o_prev is not None:
        kern = functools.partial(_drop_arg, kern, len(args) - 1)
    return pl.pallas_call(
        kern,
        grid=(batch, n_g, n_tt),
        in_specs=in_specs,
        out_specs=[pl.BlockSpec((tt_rows, 4 * d), lambda b, g, t: (row(b, g, t), g)),
                   pl.BlockSpec((1, 4, d, d), lambda b, g, t: (b, g, 0, 0))],
        out_shape=[jax.ShapeDtypeStruct((n_tok, val_dim), BF16),
                   jax.ShapeDtypeStruct((batch, 4 * n_g, d, d), F32)],
        scratch_shapes=[pltpu.VMEM((8 + tt_rows, 2 * d), F32), pltpu.VMEM((8 + tt_rows, 2 * d), F32),
                        pltpu.VMEM((8 + tt_rows, 4 * d), F32), pltpu.VMEM((4, d, d), F32)],
        input_output_aliases=aliases,
        compiler_params=_cparams("parallel", "parallel", "arbitrary"),
        name=name,
    )(*args)


def _drop_arg(kern, pos, *refs):
    return kern(*refs[:pos], *refs[pos + 1:])


def _sb_kernel(q_ref, k_ref, v_ref, o_ref, kb_s, vb_s, oacc_s, carry_s, *, tq, tk, q_start, grp):
    qi = pl.program_id(2)
    d = HEAD_DIM
    rows = grp * tq

    @pl.when(qi == 0)
    def _():
        kb_s[...] = k_ref[...].astype(BF16)
        vb_s[...] = v_ref[...].astype(BF16)

    q = q_ref[...]
    qs = jnp.concatenate([q[:, h * d:(h + 1) * d] for h in range(grp)], axis=0)
    oacc_s[...] = jnp.zeros_like(oacc_s)
    carry_s[...] = jnp.zeros_like(carry_s)

    q_lo = q_start + qi * tq
    q_hi = q_lo + tq - 1
    n_kt = (q_hi + tk - 1) // tk
    n_full = q_lo // tk

    u = (lax.broadcasted_iota(I32, (tk, tk), 0) > lax.broadcasted_iota(I32, (tk, tk), 1)).astype(BF16)
    u2 = jnp.concatenate([u, u], axis=0)
    qpos = q_lo + (lax.broadcasted_iota(I32, (rows, tk), 0) & (tq - 1))
    lane = lax.broadcasted_iota(I32, (rows, tk), 1)
    c2 = d ** -0.5 * 1.4426950408889634
    sign = jnp.uint32(0x80000000)

    def tile(kt, masked):
        k0 = pl.multiple_of(kt * tk, tk)
        kblk = kb_s[pl.ds(k0, tk), :]
        vblk = vb_s[pl.ds(k0, tk), :]
        z = lax.dot_general(qs, kblk, (((1,), (1,)), ((), ())), preferred_element_type=F32) * c2
        neg_abs = lax.bitcast_convert_type(lax.bitcast_convert_type(z, U32) | sign, F32)
        t = jnp.maximum(z, 0.0) + jnp.log2(1.0 + jnp.exp2(neg_abs))
        if masked:
            m = (k0 + lane) < qpos
            t = jnp.where(m, t, 0.0)
        hi, lo = _split_bf16(t)
        later = jnp.dot(jnp.concatenate([hi, lo], axis=1), u2, preferred_element_type=F32)
        carry = carry_s[...]
        a = jnp.exp2((z - t) - later - jnp.concatenate([carry] * (tk // d), axis=1))
        if masked:
            a = jnp.where(m, a, 0.0)
        oacc_s[...] += jnp.dot(a.astype(BF16), vblk, preferred_element_type=F32)
        carry_s[...] = carry + jnp.broadcast_to(later[:, 0:1] + t[:, 0:1], carry.shape)

    def masked_body(it, c):
        tile(n_kt - 1 - it, True)
        return c

    def full_body(it, c):
        tile(n_full - 1 - it, False)
        return c

    lax.fori_loop(0, n_kt - n_full, masked_body, 0)
    lax.fori_loop(0, n_full, full_body, 0)
    for h in range(grp):
        o_ref[:, h * d:(h + 1) * d] = oacc_s[h * tq:(h + 1) * tq, :].astype(o_ref.dtype)


def _stick_breaking(q, k_arr, v_arr, o_prev, *, batch, seq, q_row_off, kv_len, kv_row_off, k_col0, v_col0,
                    n_kv, grp, q_start, tq, name):
    n_tok, qd = q.shape
    d = HEAD_DIM
    tk = SB_KEY_TILE
    assert seq % tq == 0 and q_row_off % tq == 0 and kv_row_off % kv_len == 0 and kv_len % tk == 0
    assert tq & (tq - 1) == 0
    assert q_start + seq <= kv_len + tk and (q_start + seq - 1 + tk - 1) // tk * tk <= kv_len
    n_q = seq // tq
    in_specs = [
        pl.BlockSpec((tq, grp * d), lambda b, g, i: (q_row_off // tq + b * n_q + i, g)),
        pl.BlockSpec((kv_len, d), lambda b, g, i: (kv_row_off // kv_len + b, k_col0 + g)),
        pl.BlockSpec((kv_len, d), lambda b, g, i: (kv_row_off // kv_len + b, v_col0 + g)),
    ]
    args = [q, k_arr, v_arr]
    kern = functools.partial(_sb_kernel, tq=tq, tk=tk, q_start=q_start, grp=grp)
    aliases = {}
    if o_prev is not None:
        in_specs.append(pl.BlockSpec(memory_space=pl.ANY))
        args.append(o_prev)
        aliases = {3: 0}
        kern = functools.partial(_drop_arg, kern, 3)
    return pl.pallas_call(
        kern,
        grid=(batch, n_kv, n_q),
        in_specs=in_specs,
        out_specs=pl.BlockSpec((tq, grp * d), lambda b, g, i: (q_row_off // tq + b * n_q + i, g)),
        out_shape=jax.ShapeDtypeStruct((n_tok, qd), BF16),
        scratch_shapes=[pltpu.VMEM((kv_len, d), BF16), pltpu.VMEM((kv_len, d), BF16),
                        pltpu.VMEM((grp * tq, d), F32), pltpu.VMEM((grp * tq, d), F32)],
        input_output_aliases=aliases,
        compiler_params=_cparams("parallel", "parallel", "arbitrary"),
        name=name,
    )(*args)


def _router_kernel(x_ref, sh_ref, sc_ref, rwt_ref, rb_ref, hn_ref, idx_ref, wts_ref, hn_s, *, n_sub):
    for g in range(n_sub):
        rows = pl.ds(g * GROUP_ROWS, GROUP_ROWS)
        hn_s[rows, :] = _rms(x_ref[rows, :]) * (1.0 + sc_ref[g:g + 1, :]) + sh_ref[g:g + 1, :]
    hn = hn_s[...]
    _store_token_tiles(hn_ref, hn)

    logits = lax.dot_general(rwt_ref[...], hn, (((1,), (1,)), ((), ())), precision=lax.Precision.HIGHEST,
                             preferred_element_type=F32)
    scores = _sigmoid(logits)
    biased = scores + rb_ref[...]
    n_e, tm = biased.shape
    per = n_e // N_GROUPS
    neg = -jnp.inf
    fsum = lambda x: jnp.sum(x, axis=0, keepdims=True)
    fmax = lambda x: jnp.max(x, axis=0, keepdims=True)

    grp_score = []
    for a in range(N_GROUPS):
        blk = biased[a * per:(a + 1) * per]
        m1 = fmax(blk)
        n_top = fsum((blk == m1).astype(F32))
        m2 = fmax(jnp.where(blk < m1, blk, neg))
        grp_score.append(m1 + jnp.where(n_top > 1.5, m1, m2))
    masked = []
    for a in range(N_GROUPS):
        rank = jnp.zeros_like(grp_score[a])
        for b in range(N_GROUPS):
            if b != a:
                ahead = (grp_score[b] > grp_score[a]) | ((grp_score[b] == grp_score[a]) & (b < a))
                rank = rank + ahead.astype(F32)
        masked.append(jnp.where(rank < TOPK_GROUPS - 0.5, biased[a * per:(a + 1) * per], neg))
    mk = jnp.concatenate(masked, axis=0)
    e_id = lax.broadcasted_iota(I32, (n_e, tm), 0).astype(F32)
    ids, ws = [], []
    for _ in range(TOP_K):
        m = fmax(mk)
        first = jnp.min(jnp.where(mk == m, e_id, float(n_e)), axis=0, keepdims=True)
        hit = e_id == first
        ws.append(fsum(jnp.where(hit, scores, 0.0)))
        ids.append(first)
        mk = jnp.where(hit, neg, mk)
    total = ws[0]
    for w in ws[1:]:
        total = total + w
    idx_ref[...] = jnp.concatenate(ids, axis=0).astype(I32)
    wts_ref[...] = jnp.concatenate([w / total * ROUTED_SCALE for w in ws], axis=0)


def _router(x, sh, sc, rw_t, rb, *, tm):
    n_tok, dm = x.shape
    n_e = rw_t.shape[0]
    n_sub = tm // GROUP_ROWS
    return pl.pallas_call(
        functools.partial(_router_kernel, n_sub=n_sub),
        grid=(n_tok // tm,),
        in_specs=[pl.BlockSpec((tm, dm), lambda i: (i, 0)),
                  pl.BlockSpec((n_sub, dm), lambda i: (i, 0)),
                  pl.BlockSpec((n_sub, dm), lambda i: (i, 0)),
                  pl.BlockSpec((n_e, dm), lambda i: (0, 0)),
                  pl.BlockSpec((n_e, 1), lambda i: (0, 0))],
        out_specs=[pl.BlockSpec((tm * (dm // 2 // LANES), LANES), lambda i: (i, 0)),
                   pl.BlockSpec((TOP_K, tm), lambda i: (0, i)),
                   pl.BlockSpec((TOP_K, tm), lambda i: (0, i))],
        out_shape=[jax.ShapeDtypeStruct((n_tok * (dm // 2 // LANES), LANES), U32),
                   jax.ShapeDtypeStruct((TOP_K, n_tok), I32),
                   jax.ShapeDtypeStruct((TOP_K, n_tok), F32)],
        scratch_shapes=[pltpu.VMEM((tm, dm), F32)],
        compiler_params=_cparams("parallel"),
        name="moe_router",
    )(x, sh, sc, rw_t, rb.reshape(n_e, 1))


IDX_ALIGN = 128
CAST_ROWS = 256
SCATTER_SHIFT = 3
SCATTER_UNROLL = 1 << SCATTER_SHIFT


def _expert_kernel(blk_e_ref, s0_ref, cnt_ref, tok_hbm, slot_hbm, hn_hbm, wgu_ref, wd_ref, *rest, bm, spt, has_prev,
                   run_slot0):
    y_hbm, tbuf, sbuf, xbuf, ybuf, wgu_s, wd_s, gsem, ssem, isem = rest[1:] if has_prev else rest
    i = pl.program_id(0)
    n_blk = pl.num_programs(0)
    cur = i % 2
    win = bm + IDX_ALIGN
    in_order = run_slot0 is not None
    tile = lambda t: pl.ds(pl.multiple_of(t * spt, spt), spt)

    def idx_copies(b):
        a0 = pl.multiple_of(s0_ref[b] & (-IDX_ALIGN), IDX_ALIGN)
        return (pltpu.make_async_copy(tok_hbm.at[pl.ds(a0, win)], tbuf.at[b % 3], isem.at[0, b % 3]),
                pltpu.make_async_copy(slot_hbm.at[pl.ds(a0, win)], sbuf.at[b % 3], isem.at[1, b % 3]))

    def idx_start(b):
        if not in_order:
            for c in idx_copies(b):
                c.start()

    def idx_wait(b):
        if not in_order:
            for c in idx_copies(b):
                c.wait()

    def run_rows(t0):
        return pl.ds(pl.multiple_of(t0 * spt, bm * spt), bm * spt)

    def gather(b, buf):
        if in_order:
            pltpu.make_async_copy(hn_hbm.at[run_rows(s0_ref[b])], xbuf.at[buf], gsem.at[buf]).start()
            return
        buf3 = b % 3
        off = s0_ref[b] & (IDX_ALIGN - 1)
        for r in range(bm):
            pltpu.make_async_copy(hn_hbm.at[tile(tbuf[buf3, off + r])], xbuf.at[buf, pl.ds(r * spt, spt)],
                                  gsem.at[buf]).start(priority=r % 2)

    def wait_gather(buf):
        pltpu.make_async_copy(hn_hbm.at[pl.ds(0, bm * spt)], xbuf.at[buf], gsem.at[buf]).wait()

    def scatter_row(buf3, off, r, queue=0):
        pltpu.make_async_copy(ybuf.at[tile(r)], y_hbm.at[tile(sbuf[buf3, off + r])], ssem.at[0]).start(priority=queue)

    def wait_scatter(n_tok):
        n8 = (n_tok >> SCATTER_SHIFT) << SCATTER_SHIFT
        rows8 = pl.multiple_of(n8 * spt, SCATTER_UNROLL)

        @pl.when(n8 > 0)
        def _():
            pltpu.make_async_copy(ybuf.at[pl.ds(0, rows8)], y_hbm.at[pl.ds(0, rows8)], ssem.at[0]).wait()

        def one_token(r, c):
            pltpu.make_async_copy(ybuf.at[pl.ds(0, spt)], y_hbm.at[pl.ds(0, spt)], ssem.at[0]).wait()
            return c

        lax.fori_loop(n8, n_tok, one_token, 0)

    @pl.when(i == 0)
    def _():
        idx_start(0)
        idx_wait(0)
        gather(0, 0)

        @pl.when(n_blk > 1)
        def _():
            idx_start(1)

    @pl.when(i + 1 < n_blk)
    def _():
        idx_wait(i + 1)
        gather(i + 1, 1 - cur)

    @pl.when(i + 2 < n_blk)
    def _():
        idx_start(i + 2)

    @pl.when((i == 0) | (blk_e_ref[i] != blk_e_ref[jnp.maximum(i - 1, 0)]))
    def _():
        for dst, src in ((wgu_s, wgu_ref), (wd_s, wd_ref)):
            step = min(CAST_ROWS, dst.shape[0])
            for r0 in range(0, dst.shape[0], step):
                dst[r0:r0 + step, :] = src[0, 0, r0:r0 + step, :].astype(BF16)

    wait_gather(cur)
    los, his = _load_token_tiles(xbuf, bm, spt, base=cur)
    half = spt * LANES
    x_lo = jnp.concatenate(los, axis=1).astype(BF16)
    x_hi = jnp.concatenate(his, axis=1).astype(BF16)
    gu = (jnp.dot(x_lo, wgu_s[:half, :], preferred_element_type=F32)
          + jnp.dot(x_hi, wgu_s[half:, :], preferred_element_type=F32))
    de = gu.shape[1] // 2
    h = (_silu(gu[:, :de]) * gu[:, de:]).astype(BF16)
    y = jnp.dot(h, wd_s[...], preferred_element_type=F32)

    @pl.when(i > 0)
    def _():
        wait_scatter(cnt_ref[jnp.maximum(i - 1, 0)])

    _store_token_tiles(ybuf, y)
    cnt = cnt_ref[i]
    if in_order:
        pltpu.make_async_copy(ybuf, y_hbm.at[run_rows(run_slot0 + s0_ref[i])], ssem.at[0]).start()
    else:
        buf3 = i % 3
        off = s0_ref[i] & (IDX_ALIGN - 1)
        n_grp = cnt >> SCATTER_SHIFT

        def grp_body(j, c):
            for u in range(SCATTER_UNROLL):
                scatter_row(buf3, off, j * SCATTER_UNROLL + u, queue=u % 2)
            return c

        def row_body(r, c):
            scatter_row(buf3, off, r)
            return c

        lax.fori_loop(0, n_grp, grp_body, 0)
        lax.fori_loop(n_grp * SCATTER_UNROLL, cnt, row_body, 0)

    @pl.when(i == n_blk - 1)
    def _():
        wait_scatter(cnt)


def _experts(hn_tiles, wgu, wd, layer, plan, y_prev, *, n_slots, bm, name, run_slot0=None):
    blk_e, s0, cnt, tok, slot = plan
    n_blk = blk_e.shape[0]
    _, _, dm, de2 = wgu.shape
    spt = dm // 2 // LANES
    any_spec = pl.BlockSpec(memory_space=pl.ANY)
    in_specs = [any_spec, any_spec, any_spec,
                pl.BlockSpec((1, 1, dm, de2), lambda i, e, s, c: (layer, e[i], 0, 0)),
                pl.BlockSpec((1, 1, de2 // 2, dm), lambda i, e, s, c: (layer, e[i], 0, 0))]
    args = [blk_e, s0, cnt, tok, slot, hn_tiles, wgu, wd]
    aliases = {}
    if y_prev is not None:
        in_specs.append(any_spec)
        args.append(y_prev)
        aliases = {len(args) - 1: 0}
    grid_spec = pltpu.PrefetchScalarGridSpec(
        num_scalar_prefetch=3,
        grid=(n_blk,),
        in_specs=in_specs,
        out_specs=any_spec,
        scratch_shapes=[pltpu.SMEM((3, bm + IDX_ALIGN), I32), pltpu.SMEM((3, bm + IDX_ALIGN), I32),
                        pltpu.VMEM((2, bm * spt, LANES), U32), pltpu.VMEM((bm * spt, LANES), U32),
                        pltpu.VMEM((dm, de2), BF16), pltpu.VMEM((de2 // 2, dm), BF16),
                        pltpu.SemaphoreType.DMA((2,)), pltpu.SemaphoreType.DMA((1,)), pltpu.SemaphoreType.DMA((2, 3))],
    )
    return pl.pallas_call(
        functools.partial(_expert_kernel, bm=bm, spt=spt, has_prev=y_prev is not None, run_slot0=run_slot0),
        grid_spec=grid_spec,
        out_shape=jax.ShapeDtypeStruct((n_slots * spt, LANES), U32),
        input_output_aliases=aliases,
        compiler_params=_cparams("arbitrary"),
        name=name,
    )(*args)


def _combine_kernel(x_ref, y_ref, w_ref, gate_ref, fn_ref, *rest, n_sub, final, n_first):
    outs, ff_s = rest[:-1], rest[-1]
    n_k = y_ref.shape[0]
    tm, dm = ff_s.shape
    half = dm // 2
    spt = half // LANES
    w = w_ref[...]
    wk = [jnp.broadcast_to(w[:, k:k + 1], (tm, LANES)) for k in range(n_k - 1)]
    for s in range(spt):
        piece = lambda k: _unpack_pair(y_ref[k, pl.ds(s, tm, stride=spt), :])
        lo, hi = piece(n_k - 1)
        for k in range(n_k - 1):
            yl, yh = piece(k)
            lo = lo + wk[k] * yl
            hi = hi + wk[k] * yh
        ff_s[:, s * LANES:(s + 1) * LANES] = lo
        ff_s[:, half + s * LANES:half + (s + 1) * LANES] = hi
    for g in range(n_sub):
        rows = pl.ds(g * GROUP_ROWS, GROUP_ROWS)
        out = x_ref[rows, :] + gate_ref[g:g + 1, :] * ff_s[rows, :]
        if final:
            out = _rms(out) * fn_ref[...]
        ff_s[rows, :] = out
    if n_first is None:
        outs[0][...] = ff_s[...]
    else:
        i = pl.program_id(0)

        @pl.when(i < n_first)
        def _():
            outs[0][...] = ff_s[...]

        @pl.when(i >= n_first)
        def _():
            outs[1][...] = ff_s[...]


def _combine(x, y_slots, wts_tok, gate, final_norm, *, tm, final, n_first_rows=None):
    n_tok, dm = x.shape
    n_k = TOP_K + 1
    n_sub = tm // GROUP_ROWS
    spt = dm // 2 // LANES
    y3 = y_slots.reshape(n_k, n_tok * spt, LANES)
    if n_first_rows is None:
        n_first = None
        out_specs = pl.BlockSpec((tm, dm), lambda i: (i, 0))
        out_shape = jax.ShapeDtypeStruct((n_tok, dm), F32)
    else:
        assert n_first_rows % tm == 0 and n_tok - n_first_rows == tm
        n_first = n_first_rows // tm
        out_specs = [pl.BlockSpec((tm, dm), lambda i: (jnp.minimum(i, n_first - 1), 0)),
                     pl.BlockSpec((tm, dm), lambda i: (0, 0))]
        out_shape = [jax.ShapeDtypeStruct((n_first_rows, dm), F32), jax.ShapeDtypeStruct((tm, dm), F32)]
    return pl.pallas_call(
        functools.partial(_combine_kernel, n_sub=n_sub, final=final, n_first=n_first),
        grid=(n_tok // tm,),
        in_specs=[pl.BlockSpec((tm, dm), lambda i: (i, 0)),
                  pl.BlockSpec((n_k, tm * spt, LANES), lambda i: (0, i, 0)),
                  pl.BlockSpec((tm, TOP_K), lambda i: (i, 0)),
                  pl.BlockSpec((n_sub, dm), lambda i: (i, 0)),
                  pl.BlockSpec((1, dm), lambda i: (0, 0))],
        out_specs=out_specs,
        out_shape=out_shape,
        scratch_shapes=[pltpu.VMEM((tm, dm), F32)],
        compiler_params=_cparams("arbitrary"),
        name="moe_combine",
    )(x, y3, wts_tok, gate, final_norm.reshape(1, dm))


def _moe_plan(idx_t, bm):
    n_tok = idx_t.shape[1]
    assert n_tok < (1 << 16)
    flat_e = idx_t.reshape(-1)
    a = flat_e.shape[0]
    _, order = lax.sort((flat_e, jnp.arange(a, dtype=I32)), num_keys=1, is_stable=True)
    tail = jnp.zeros((bm + 2 * IDX_ALIGN,), I32)
    tok = jnp.concatenate([order % n_tok, tail])
    slot = jnp.concatenate([order, tail])
    e_ids = jnp.arange(N_EXPERTS, dtype=I32)
    counts = jnp.sum((flat_e[:, None] == e_ids[None, :]).astype(I32), axis=0)
    start = jnp.cumsum(counts) - counts
    nblk_e = (counts + bm - 1) // bm
    blk_end = jnp.cumsum(nblk_e)
    n_blk = -(-(a + N_EXPERTS * (bm - 1)) // bm)
    b = jnp.arange(n_blk, dtype=I32)
    blk_e = jnp.minimum(jnp.sum((b[:, None] >= blk_end[None, :]).astype(I32), axis=1), N_EXPERTS - 1)
    onehot = (blk_e[:, None] == e_ids[None, :]).astype(I32)
    pick = lambda v: jnp.sum(onehot * v[None, :], axis=1)
    j = b - pick(blk_end - nblk_e)
    s0 = jnp.minimum(pick(start) + j * bm, a)
    cnt = jnp.clip(pick(counts) - j * bm, 0, bm)
    return blk_e, s0, cnt, tok, slot


def _shared_plan(n_tok, bm):
    assert n_tok % bm == 0
    n_blk = n_tok // bm
    tail = jnp.zeros((bm + 2 * IDX_ALIGN,), I32)
    tok = jnp.concatenate([jnp.arange(n_tok, dtype=I32), tail])
    slot = jnp.concatenate([TOP_K * n_tok + jnp.arange(n_tok, dtype=I32), tail])
    return (jnp.zeros((n_blk,), I32), jnp.arange(n_blk, dtype=I32) * bm, jnp.full((n_blk,), bm, I32), tok, slot)


def _moe(x, sh, sc, gate, router_w, router_bias, w_gate_up, w_down, ws_gate_up, ws_down, final_norm, *, layer, final,
         tm, n_first_rows=None):
    n_tok, dm = x.shape
    hn_tiles, idx_t, wts_t = _router(x, sh, sc, router_w.T, router_bias, tm=tm)
    n_slots = (TOP_K + 1) * n_tok
    y_slots = _experts(hn_tiles, w_gate_up, w_down, layer, _moe_plan(idx_t, MOE_BM), None, n_slots=n_slots,
                       bm=MOE_BM, name="moe_experts")
    y_slots = _experts(hn_tiles, ws_gate_up[:, None], ws_down[:, None], layer, _shared_plan(n_tok, MOE_BM), y_slots,
                       n_slots=n_slots, bm=MOE_BM, name="moe_shared_expert", run_slot0=TOP_K * n_tok)
    return _combine(x, y_slots, wts_t.T, gate, final_norm, tm=tm, final=final, n_first_rows=n_first_rows)


def _group_rows(mod, bp, tp, bs, ts):
    return jnp.concatenate([jnp.repeat(mod[:bp], tp // GROUP_ROWS, axis=0),
                            jnp.repeat(mod[bp:bp + bs], ts // GROUP_ROWS, axis=0)], axis=0)


def kernel(x_prompt, x_sample, c_prompt, c_sample, state_delta, state_conv, cache_k, cache_v, ada_w, ada_b, a_w_in, a_conv_w, a_a_log, a_dt_bias, a_o_norm, a_w_out, kv_ada_w, kv_ada_b, w_kv, b_w_q, b_w_out, router_w, router_bias, w_gate_up, w_down, ws_gate_up, ws_down, final_norm):
    bp, tp, dm = x_prompt.shape
    bs, ts, _ = x_sample.shape
    n_a = a_w_in.shape[0]
    depth = ada_w.shape[0]
    assert n_a == 1 and depth == 2, "one DeltaNet layer followed by one attention layer"
    assert ts == GROUP_ROWS and tp % GROUP_ROWS == 0
    n_p, n_s = bp * tp, bs * ts
    n_tok = n_p + n_s
    d = HEAD_DIM
    hv = state_delta.shape[2]
    val_dim = hv * d
    key_dim = val_dim // 2
    conv_dim = 2 * key_dim + val_dim
    past = cache_k.shape[1]
    n_kv = cache_k.shape[2]
    grp = b_w_q.shape[2] // (n_kv * d)
    tm_big = 1280 if n_tok % 1280 == 0 else 256
    tm_tok = 256

    x = jnp.concatenate([x_prompt.reshape(n_p, dm), x_sample.reshape(n_s, dm)], axis=0)
    c_rows = 16
    c_all = jnp.concatenate([c_prompt, c_sample, jnp.zeros((c_rows - bp - bs, dm), F32)], axis=0)
    groups = lambda m: _group_rows(m, bp, tp, bs, ts)

    def ada(w, b, layer=None):
        return _linear(c_all, w, tm=c_rows, tn=1024, out_dtype=F32, silu_in=True, bias=b, layer=layer, name="adaln")

    mod = ada(ada_w, ada_b[0], layer=0)
    sh_m, sc_m, gt_m, sh_f, sc_f, gt_f = [groups(m) for m in jnp.split(mod, 6, axis=-1)]
    w_in = a_w_in[0].astype(BF16)
    n_main = conv_dim + val_dim
    qkvz = _linear(x, w_in, tm=tm_big, tn=1024, out_dtype=F32, norm=True, mod=(sh_m, sc_m), n_cols=n_main,
                   name="gdn_in_proj")
    gate_tn = 128
    assert n_main % gate_tn == 0 and 2 * hv <= gate_tn
    ba = _linear(x, w_in, tm=tm_big, tn=gate_tn, out_dtype=F32, norm=True, mod=(sh_m, sc_m), n_cols=gate_tn,
                 col_blk0=n_main // gate_tn, name="gdn_gate_proj")[:, :2 * hv]
    n_g = hv // 4
    ba_g = ba.reshape(n_tok, 2, n_g, 4).transpose(2, 0, 1, 3).reshape(n_g, n_tok, 8)
    pad4 = jnp.zeros((n_g, 4), F32)
    al_g = jnp.concatenate([pad4, a_a_log[0].reshape(n_g, 4)], axis=1).reshape(n_g, 1, 8)
    dtb_g = jnp.concatenate([pad4, a_dt_bias[0].reshape(n_g, 4)], axis=1).reshape(n_g, 1, 8)
    gdn = functools.partial(_gated_deltanet, qkvz, ba_g, al_g, dtb_g, a_conv_w[0], key_dim=key_dim, val_dim=val_dim)
    o_gdn, delta_p = gdn(jnp.zeros((bp, CONV_W - 1, conv_dim), F32), jnp.zeros((bp, hv, d, d), F32), a_o_norm[0], None,
                         batch=bp, seq=tp, row_off=0, tt_rows=512, chunk=CHUNK, name="gdn_prompt")
    o_gdn, delta_s = gdn(state_conv[0], state_delta[0], a_o_norm[0], o_gdn,
                         batch=bs, seq=ts, row_off=n_p, tt_rows=ts, chunk=min(CHUNK, ts), name="gdn_sample")
    last_rows = lambda r0, t: lax.slice(qkvz, (r0 + t - (CONV_W - 1), 0), (r0 + t, conv_dim))
    conv_p = jnp.stack([last_rows(b * tp, tp) for b in range(bp)])
    conv_s = jnp.stack([last_rows(n_p + b * ts, ts) for b in range(bs)])
    x = _linear(o_gdn, a_w_out[0].astype(BF16), tm=tm_big, tn=512, out_dtype=F32, res=x, gate=gt_m, name="gdn_out_proj")
    x = _moe(x, sh_f, sc_f, gt_f, router_w[0], router_bias[0], w_gate_up, w_down, ws_gate_up, ws_down,
             final_norm, layer=0, final=False, tm=tm_tok)

    mod = ada(ada_w, ada_b[1], layer=1)
    sh_m, sc_m, gt_m, sh_f, sc_f, gt_f = [groups(m) for m in jnp.split(mod, 6, axis=-1)]
    kv_mod = ada(kv_ada_w, kv_ada_b)
    kv_sh, kv_sc = [groups(m) for m in jnp.split(kv_mod, 2, axis=-1)]
    kv = _linear(x, w_kv.astype(BF16), tm=tm_big, tn=1024, out_dtype=F32, norm=True, mod=(kv_sh, kv_sc), name="kv_proj")
    q = _linear(x, b_w_q[0].astype(BF16), tm=tm_big, tn=1024, out_dtype=BF16, norm=True, mod=(sh_m, sc_m), name="q_proj")
    kd = n_kv * d
    k_p = kv[:n_p, :kd].reshape(bp, tp, n_kv, d)
    v_p = kv[:n_p, kd:].reshape(bp, tp, n_kv, d)
    k_s = kv[n_p:, :kd].reshape(bs, ts, n_kv, d)
    v_s = kv[n_p:, kd:].reshape(bs, ts, n_kv, d)
    kv_len_s = -(-(past + ts) // SB_KEY_TILE) * SB_KEY_TILE
    tail = jnp.zeros((bs, kv_len_s - past - ts, kd), F32)
    k_all = jnp.concatenate([cache_k.reshape(bs, past, kd), k_s.reshape(bs, ts, kd), tail], axis=1).reshape(-1, kd)
    v_all = jnp.concatenate([cache_v.reshape(bs, past, kd), v_s.reshape(bs, ts, kd), tail], axis=1).reshape(-1, kd)
    o_att = _stick_breaking(q, kv, kv, None, batch=bp, seq=tp, q_row_off=0, kv_len=tp, kv_row_off=0, k_col0=0,
                            v_col0=n_kv, n_kv=n_kv, grp=grp, q_start=0, tq=256, name="attn_prompt")
    o_att = _stick_breaking(q, k_all, v_all, o_att, batch=bs, seq=ts, q_row_off=n_p, kv_len=kv_len_s, kv_row_off=0,
                            k_col0=0, v_col0=0, n_kv=n_kv, grp=grp, q_start=past, tq=ts, name="attn_sample")
    x = _linear(o_att, b_w_out[0].astype(BF16), tm=tm_big, tn=1024, out_dtype=F32, res=x, gate=gt_m, name="attn_out_proj")
    assert n_s == tm_tok
    y_p, y_s = _moe(x, sh_f, sc_f, gt_f, router_w[1], router_bias[1], w_gate_up, w_down, ws_gate_up, ws_down,
                    final_norm, layer=1, final=True, tm=tm_tok, n_first_rows=n_p)
    y_p = y_p.reshape(bp, tp, dm)
    y_s = y_s.reshape(bs, ts, dm)
    return (y_p, y_s, k_p, v_p, k_s, v_s, delta_p[None], conv_p[None], delta_s[None], conv_s[None])
```

```python
import functools

import jax
import jax.numpy as jnp
from jax import lax
from jax.experimental import pallas as pl
from jax.experimental.pallas import tpu as pltpu

F32, BF16, I32, U32 = jnp.float32, jnp.bfloat16, jnp.int32, jnp.uint32

NORM_EPS = 1e-6
CHUNK = 64
CONV_W = 4
N_EXPERTS = 64
TOP_K = 8
N_GROUPS = 8
TOPK_GROUPS = 4
ROUTED_SCALE = 2.5
HEAD_DIM = 128
LANES = 128
GROUP_ROWS = 32
MOE_BM = 512
MOE_SHARED_BM = 256
SB_KEY_TILE = 256
VMEM_LIMIT_BYTES = 56 * 1024 * 1024


def _cparams(*sem):
    return pltpu.CompilerParams(dimension_semantics=sem, vmem_limit_bytes=VMEM_LIMIT_BYTES)


def _sigmoid(x):
    return 1.0 / (1.0 + jnp.exp(-x))


def _silu(x):
    return x * _sigmoid(x)


def _softplus(x):
    return jnp.maximum(x, 0.0) + jnp.log(1.0 + jnp.exp(-jnp.abs(x)))


def _rms(x):
    return x * lax.rsqrt(jnp.mean(x * x, axis=-1, keepdims=True) + NORM_EPS)


def _pack_pair(lo, hi):
    lo_bits = lax.bitcast_convert_type(lo.astype(BF16).astype(F32), U32) >> 16
    hi_bits = lax.bitcast_convert_type(hi.astype(BF16).astype(F32), U32) & jnp.uint32(0xFFFF0000)
    return lo_bits | hi_bits


def _unpack_pair(w):
    lo = lax.bitcast_convert_type(w << 16, F32)
    hi = lax.bitcast_convert_type(w & jnp.uint32(0xFFFF0000), F32)
    return lo, hi


def _store_token_tiles(ref, x, base=None):
    n, dm = x.shape
    half = dm // 2
    spt = half // LANES
    words = _pack_pair(x[:, :half], x[:, half:])
    for s in range(spt):
        piece = words[:, s * LANES:(s + 1) * LANES]
        if base is None:
            ref[pl.ds(s, n, stride=spt), :] = piece
        else:
            ref[base, pl.ds(s, n, stride=spt), :] = piece


def _load_token_tiles(ref, n, spt, base=None):
    los, his = [], []
    for s in range(spt):
        w = ref[pl.ds(s, n, stride=spt), :] if base is None else ref[base, pl.ds(s, n, stride=spt), :]
        lo, hi = _unpack_pair(w)
        los.append(lo)
        his.append(hi)
    return los, his


def _linear_kernel(*refs, norm, silu_in, has_mod, has_bias, has_res, prologue, n_sub):
    it = iter(refs)
    x_ref, w_ref = next(it), next(it)
    sh_ref = sc_ref = b_ref = res_ref = gate_ref = xs_ref = None
    if has_mod:
        sh_ref, sc_ref = next(it), next(it)
    if has_bias:
        b_ref = next(it)
    if has_res:
        res_ref, gate_ref = next(it), next(it)
    o_ref = next(it)
    if prologue:
        xs_ref = next(it)

        @pl.when(pl.program_id(1) == 0)
        def _():
            def prep(x):
                x = x.astype(F32)
                if silu_in:
                    x = _silu(x)
                if norm:
                    x = _rms(x)
                return x

            if has_mod:
                for g in range(n_sub):
                    rows = pl.ds(g * GROUP_ROWS, GROUP_ROWS)
                    xg = prep(x_ref[rows, :])
                    xg = xg * (1.0 + sc_ref[g:g + 1, :]) + sh_ref[g:g + 1, :]
                    xs_ref[rows, :] = xg.astype(BF16)
            else:
                xs_ref[...] = prep(x_ref[...]).astype(BF16)

        xs = xs_ref[...]
    else:
        xs = x_ref[...]
    acc = jnp.dot(xs, w_ref[...].astype(BF16), preferred_element_type=F32)
    if has_bias:
        acc = acc + b_ref[...]
    if has_res:
        for g in range(n_sub):
            rows = pl.ds(g * GROUP_ROWS, GROUP_ROWS)
            part = acc[g * GROUP_ROWS:(g + 1) * GROUP_ROWS, :] * gate_ref[g:g + 1, :]
            o_ref[rows, :] = (res_ref[rows, :] + part).astype(o_ref.dtype)
    else:
        o_ref[...] = acc.astype(o_ref.dtype)


def _linear(x, w, *, tm, tn, out_dtype, norm=False, silu_in=False, mod=None, bias=None, res=None, gate=None,
            n_cols=None, col_blk0=0, layer=None, name="linear"):
    m, k = x.shape
    n = w.shape[-1] if n_cols is None else n_cols
    assert m % tm == 0 and n % tn == 0, (m, tm, n, tn)
    has_mod, has_bias, has_res = mod is not None, bias is not None, res is not None
    prologue = norm or silu_in or has_mod or x.dtype != BF16
    n_sub = tm // GROUP_ROWS if (has_mod or has_res) else 0
    if layer is None:
        w_spec = pl.BlockSpec((k, tn), lambda i, j: (0, col_blk0 + j))
    else:
        w_spec = pl.BlockSpec((None, k, tn), lambda i, j: (layer, 0, col_blk0 + j))
    in_specs = [pl.BlockSpec((tm, k), lambda i, j: (i, 0)), w_spec]
    args = [x, w]
    if has_mod:
        in_specs += [pl.BlockSpec((n_sub, k), lambda i, j: (i, 0))] * 2
        args += list(mod)
    if has_bias:
        in_specs.append(pl.BlockSpec((1, tn), lambda i, j: (0, j)))
        args.append(bias.reshape(1, n))
    if has_res:
        in_specs += [pl.BlockSpec((tm, tn), lambda i, j: (i, j)), pl.BlockSpec((n_sub, tn), lambda i, j: (i, j))]
        args += [res, gate]
    return pl.pallas_call(
        functools.partial(_linear_kernel, norm=norm, silu_in=silu_in, has_mod=has_mod, has_bias=has_bias,
                          has_res=has_res, prologue=prologue, n_sub=n_sub),
        grid=(m // tm, n // tn),
        in_specs=in_specs,
        out_specs=pl.BlockSpec((tm, tn), lambda i, j: (i, j)),
        out_shape=jax.ShapeDtypeStruct((m, n), out_dtype),
        scratch_shapes=[pltpu.VMEM((tm, k), BF16)] if prologue else [],
        compiler_params=_cparams("parallel", "arbitrary"),
        name=name,
    )(*args)


def _split_bf16(a):
    hi = a.astype(BF16)
    lo = (a - hi.astype(F32)).astype(BF16)
    return hi, lo


def _bmm(a, b):
    return jnp.einsum("bij,bjk->bik", a, b, preferred_element_type=F32)


def _bmm3(a, b):
    ah, al = _split_bf16(a)
    bh, bl = _split_bf16(b)
    m = a.shape[1]
    both = _bmm(jnp.concatenate([ah, al], axis=1), bh)
    return both[:, :m] + both[:, m:] + _bmm(ah, bl)


def _bmm1(a, b):
    return _bmm(a.astype(BF16), b.astype(BF16))


def _unit_lower_inverse(a, c):
    r = lax.broadcasted_iota(I32, (c, c), 0)
    q = lax.broadcasted_iota(I32, (c, c), 1)
    same = lambda s: (r ^ q) < s
    eye = (r == q).astype(F32)
    a0 = jnp.where(same(8), a, 0.0)
    a2 = _bmm1(a0, a0)
    a4 = _bmm1(a2, a2)
    t = eye - a0
    t = t + _bmm1(t, a2)
    t = t + _bmm1(t, a4)
    s = 8
    while s < c:
        a_off = jnp.where(same(2 * s) & jnp.logical_not(same(s)), a, 0.0)
        t = t - _bmm1(_bmm1(t, a_off), t)
        s *= 2
    return t


def _unit_lower_solve(a, rhs, c):
    t = _unit_lower_inverse(a, c).astype(BF16)
    x0 = _bmm(t, rhs.astype(BF16))
    resid = rhs - (x0 + _bmm3(a, x0))
    return x0 + _bmm(t, resid.astype(BF16))


def _gdn_kernel(q_ref, k_ref, v_ref, z_ref, ba_ref, al_ref, dtb_ref, cwq_ref, cwk_ref, cwv_ref,
                c0q_ref, c0k_ref, c0v_ref, s0_ref, on_ref, o_ref, sfin_ref,
                xq_s, xk_s, xv_s, state_s, *, tt_rows, chunk):
    tt = pl.program_id(2)
    n_tt = pl.num_programs(2)
    d = HEAD_DIM
    nc = tt_rows // chunk
    halo = CONV_W - 1

    @pl.when(tt == 0)
    def _():
        state_s[...] = s0_ref[0]
        xq_s[8 - halo:8, :] = c0q_ref[0]
        xk_s[8 - halo:8, :] = c0k_ref[0]
        xv_s[8 - halo:8, :] = c0v_ref[0]

    def conv_silu(x_ref, xs, cw_ref):
        xs[8:8 + tt_rows, :] = x_ref[...]
        acc = xs[8 - halo:8 - halo + tt_rows, :] * cw_ref[0:1, :]
        for i in range(1, CONV_W):
            acc = acc + xs[8 - halo + i:8 - halo + i + tt_rows, :] * cw_ref[i:i + 1, :]
        xs[8 - halo:8, :] = xs[8 + tt_rows - halo:8 + tt_rows, :]
        return _silu(acc)

    qc = conv_silu(q_ref, xq_s, cwq_ref)
    kc = conv_silu(k_ref, xk_s, cwk_ref)
    vc = conv_silu(v_ref, xv_s, cwv_ref)

    def l2(x):
        return x * lax.rsqrt(jnp.sum(x * x, axis=-1, keepdims=True) + NORM_EPS)

    qh = [(l2(qc[:, j * d:(j + 1) * d]) * d ** -0.5).reshape(nc, chunk, d) for j in range(2)]
    kh = [l2(kc[:, j * d:(j + 1) * d]).reshape(nc, chunk, d) for j in range(2)]
    vh = [vc[:, h * d:(h + 1) * d].reshape(nc, chunk, d) for h in range(4)]

    def heads4(per_key_head):
        a, b = per_key_head
        st = jnp.stack([a, a, b, b], axis=1)
        return st.reshape((nc * 4,) + st.shape[2:])

    ba = ba_ref[0]
    beta_all = _sigmoid(ba)
    g_all = -jnp.exp(al_ref[0]) * _softplus(ba + dtb_ref[0])
    mm = lambda a, b: jnp.dot(a, b, preferred_element_type=F32)

    def col4(x, lane0):
        cols = [x[:, lane0 + h:lane0 + h + 1].reshape(nc, chunk, 1) for h in range(4)]
        return jnp.stack(cols, axis=1).reshape(nc * 4, chunk, 1)

    beta = col4(beta_all, 0)
    g_col = col4(g_all, 4)
    nb = nc * 4

    r = lax.broadcasted_iota(I32, (chunk, chunk), 0)
    q_i = lax.broadcasted_iota(I32, (chunk, chunk), 1)
    eye, incl, strict = (r == q_i), (r >= q_i), (r > q_i)
    g_b = jnp.broadcast_to(g_col, (nb, chunk, chunk))
    g_t = jnp.sum(jnp.where(eye, g_b, 0.0), axis=1, keepdims=True)
    gc_t = jnp.sum(jnp.where(r <= q_i, g_b, 0.0), axis=1, keepdims=True)
    gc = jnp.sum(jnp.where(incl, jnp.broadcast_to(g_t, (nb, chunk, chunk)), 0.0), axis=2, keepdims=True)
    decay = jnp.where(incl, jnp.exp(jnp.where(incl, gc - gc_t, 0.0)), 0.0)

    k2 = jnp.concatenate(kh, axis=0).astype(BF16)
    q2 = jnp.concatenate(qh, axis=0).astype(BF16)
    kk = jnp.einsum("bid,bjd->bij", k2, k2, preferred_element_type=F32)
    qk = jnp.einsum("bid,bjd->bij", q2, k2, preferred_element_type=F32)
    kk4 = heads4((kk[:nc], kk[nc:]))
    qk4 = heads4((qk[:nc], qk[nc:]))
    k4 = heads4(kh)
    q4 = heads4(qh)
    v4 = jnp.stack(vh, axis=1).reshape(nb, chunk, d)

    a_mat = jnp.where(strict, kk4 * decay * beta, 0.0)
    attn = (qk4 * decay).astype(BF16)
    eg = jnp.exp(gc)
    g_last = gc[:, chunk - 1:chunk, :]
    rhs = jnp.concatenate([v4 * beta, k4 * (beta * eg)], axis=-1)
    sol = _unit_lower_solve(a_mat, rhs, chunk)
    u_all, w_all = sol[:, :, :d], sol[:, :, d:].astype(BF16)
    qd_all = (q4 * eg).astype(BF16)
    kt_all = k4 * jnp.exp(g_last - gc)
    egl = jnp.exp(g_last)

    on = on_ref[...]
    s = [state_s[h] for h in range(4)]
    for c in range(nc):
        for h in range(4):
            b = c * 4 + h
            sb = s[h].astype(BF16)
            v_new = u_all[b] - mm(w_all[b], sb)
            o = mm(qd_all[b], sb) + mm(attn[b], v_new.astype(BF16))
            s[h] = s[h] * egl[b] + mm(kt_all[b].T.astype(BF16), v_new.astype(BF16))
            rows = pl.ds(c * chunk, chunk)
            z = z_ref[rows, h * d:(h + 1) * d]
            o_ref[rows, h * d:(h + 1) * d] = (_rms(o) * on * _silu(z)).astype(o_ref.dtype)
    for h in range(4):
        state_s[h] = s[h]

    @pl.when(tt == n_tt - 1)
    def _():
        for h in range(4):
            sfin_ref[0, h] = s[h]


def _gated_deltanet(qkvz, ba_g, al_g, dtb_g, conv_w, conv0, s0, o_norm, o_prev, *, batch, seq, row_off, tt_rows,
                    chunk, key_dim, val_dim, name):
    n_tok = qkvz.shape[0]
    d = HEAD_DIM
    n_g = key_dim // (2 * d)
    assert val_dim == 2 * key_dim and seq % tt_rows == 0 and row_off % tt_rows == 0 and tt_rows % chunk == 0
    n_tt = seq // tt_rows
    rb0 = row_off // tt_rows
    kq, kv_ = key_dim // (2 * d), (2 * key_dim) // (4 * d)

    row = lambda b, g, t: rb0 + b * n_tt + t
    in_specs = [
        pl.BlockSpec((tt_rows, 2 * d), lambda b, g, t: (row(b, g, t), g)),
        pl.BlockSpec((tt_rows, 2 * d), lambda b, g, t: (row(b, g, t), kq + g)),
        pl.BlockSpec((tt_rows, 4 * d), lambda b, g, t: (row(b, g, t), kv_ + g)),
        pl.BlockSpec((tt_rows, 4 * d), lambda b, g, t: (row(b, g, t), 2 * kv_ + g)),
        pl.BlockSpec((1, tt_rows, 8), lambda b, g, t: (g, row(b, g, t), 0)),
        pl.BlockSpec((1, 1, 8), lambda b, g, t: (g, 0, 0)),
        pl.BlockSpec((1, 1, 8), lambda b, g, t: (g, 0, 0)),
        pl.BlockSpec((CONV_W, 2 * d), lambda b, g, t: (0, g)),
        pl.BlockSpec((CONV_W, 2 * d), lambda b, g, t: (0, kq + g)),
        pl.BlockSpec((CONV_W, 4 * d), lambda b, g, t: (0, kv_ + g)),
        pl.BlockSpec((1, CONV_W - 1, 2 * d), lambda b, g, t: (b, 0, g)),
        pl.BlockSpec((1, CONV_W - 1, 2 * d), lambda b, g, t: (b, 0, kq + g)),
        pl.BlockSpec((1, CONV_W - 1, 4 * d), lambda b, g, t: (b, 0, kv_ + g)),
        pl.BlockSpec((1, 4, d, d), lambda b, g, t: (b, g, 0, 0)),
        pl.BlockSpec((1, d), lambda b, g, t: (0, 0)),
    ]
    args = [qkvz, qkvz, qkvz, qkvz, ba_g, al_g, dtb_g, conv_w, conv_w, conv_w, conv0, conv0, conv0, s0,
            o_norm.reshape(1, d)]
    aliases = {}
    if o_prev is not None:
        in_specs.append(pl.BlockSpec(memory_space=pl.ANY))
        args.append(o_prev)
        aliases = {len(args) - 1: 0}
    kern = functools.partial(_gdn_kernel, tt_rows=tt_rows, chunk=chunk)
    if o_prev is not None:
        kern = functools.partial(_drop_arg, kern, len(args) - 1)
    return pl.pallas_call(
        kern,
        grid=(batch, n_g, n_tt),
        in_specs=in_specs,
        out_specs=[pl.BlockSpec((tt_rows, 4 * d), lambda b, g, t: (row(b, g, t), g)),
                   pl.BlockSpec((1, 4, d, d), lambda b, g, t: (b, g, 0, 0))],
        out_shape=[jax.ShapeDtypeStruct((n_tok, val_dim), BF16),
                   jax.ShapeDtypeStruct((batch, 4 * n_g, d, d), F32)],
        scratch_shapes=[pltpu.VMEM((8 + tt_rows, 2 * d), F32), pltpu.VMEM((8 + tt_rows, 2 * d), F32),
                        pltpu.VMEM((8 + tt_rows, 4 * d), F32), pltpu.VMEM((4, d, d), F32)],
        input_output_aliases=aliases,
        compiler_params=_cparams("parallel", "parallel", "arbitrary"),
        name=name,
    )(*args)


def _drop_arg(kern, pos, *refs):
    return kern(*refs[:pos], *refs[pos + 1:])


def _sb_kernel(q_ref, k_ref, v_ref, o_ref, kb_s, vb_s, oacc_s, carry_s, *, tq, tk, q_start, grp):
    qi = pl.program_id(2)
    d = HEAD_DIM
    rows = grp * tq

    @pl.when(qi == 0)
    def _():
        kb_s[...] = k_ref[...].astype(BF16)
        vb_s[...] = v_ref[...].astype(BF16)

    q = q_ref[...]
    qs = jnp.concatenate([q[:, h * d:(h + 1) * d] for h in range(grp)], axis=0)
    oacc_s[...] = jnp.zeros_like(oacc_s)
    carry_s[...] = jnp.zeros_like(carry_s)

    q_lo = q_start + qi * tq
    q_hi = q_lo + tq - 1
    n_kt = (q_hi + tk - 1) // tk
    n_full = q_lo // tk

    u = (lax.broadcasted_iota(I32, (tk, tk), 0) > lax.broadcasted_iota(I32, (tk, tk), 1)).astype(BF16)
    u2 = jnp.concatenate([u, u], axis=0)
    qpos = q_lo + (lax.broadcasted_iota(I32, (rows, tk), 0) & (tq - 1))
    lane = lax.broadcasted_iota(I32, (rows, tk), 1)
    c2 = d ** -0.5 * 1.4426950408889634
    sign = jnp.uint32(0x80000000)

    def tile(kt, masked):
        k0 = pl.multiple_of(kt * tk, tk)
        kblk = kb_s[pl.ds(k0, tk), :]
        vblk = vb_s[pl.ds(k0, tk), :]
        z = lax.dot_general(qs, kblk, (((1,), (1,)), ((), ())), preferred_element_type=F32) * c2
        neg_abs = lax.bitcast_convert_type(lax.bitcast_convert_type(z, U32) | sign, F32)
        t = jnp.maximum(z, 0.0) + jnp.log2(1.0 + jnp.exp2(neg_abs))
        if masked:
            m = (k0 + lane) < qpos
            t = jnp.where(m, t, 0.0)
        hi, lo = _split_bf16(t)
        later = jnp.dot(jnp.concatenate([hi, lo], axis=1), u2, preferred_element_type=F32)
        carry = carry_s[...]
        a = jnp.exp2((z - t) - later - jnp.concatenate([carry] * (tk // d), axis=1))
        if masked:
            a = jnp.where(m, a, 0.0)
        oacc_s[...] += jnp.dot(a.astype(BF16), vblk, preferred_element_type=F32)
        carry_s[...] = carry + jnp.broadcast_to(later[:, 0:1] + t[:, 0:1], carry.shape)

    def masked_body(it, c):
        tile(n_kt - 1 - it, True)
        return c

    def full_body(it, c):
        tile(n_full - 1 - it, False)
        return c

    lax.fori_loop(0, n_kt - n_full, masked_body, 0)
    lax.fori_loop(0, n_full, full_body, 0)
    for h in range(grp):
        o_ref[:, h * d:(h + 1) * d] = oacc_s[h * tq:(h + 1) * tq, :].astype(o_ref.dtype)


def _stick_breaking(q, k_arr, v_arr, o_prev, *, batch, seq, q_row_off, kv_len, kv_row_off, k_col0, v_col0,
                    n_kv, grp, q_start, tq, name):
    n_tok, qd = q.shape
    d = HEAD_DIM
    tk = SB_KEY_TILE
    assert seq % tq == 0 and q_row_off % tq == 0 and kv_row_off % kv_len == 0 and kv_len % tk == 0
    assert tq & (tq - 1) == 0
    assert q_start + seq <= kv_len + tk and (q_start + seq - 1 + tk - 1) // tk * tk <= kv_len
    n_q = seq // tq
    in_specs = [
        pl.BlockSpec((tq, grp * d), lambda b, g, i: (q_row_off // tq + b * n_q + i, g)),
        pl.BlockSpec((kv_len, d), lambda b, g, i: (kv_row_off // kv_len + b, k_col0 + g)),
        pl.BlockSpec((kv_len, d), lambda b, g, i: (kv_row_off // kv_len + b, v_col0 + g)),
    ]
    args = [q, k_arr, v_arr]
    kern = functools.partial(_sb_kernel, tq=tq, tk=tk, q_start=q_start, grp=grp)
    aliases = {}
    if o_prev is not None:
        in_specs.append(pl.BlockSpec(memory_space=pl.ANY))
        args.append(o_prev)
        aliases = {3: 0}
        kern = functools.partial(_drop_arg, kern, 3)
    return pl.pallas_call(
        kern,
        grid=(batch, n_kv, n_q),
        in_specs=in_specs,
        out_specs=pl.BlockSpec((tq, grp * d), lambda b, g, i: (q_row_off // tq + b * n_q + i, g)),
        out_shape=jax.ShapeDtypeStruct((n_tok, qd), BF16),
        scratch_shapes=[pltpu.VMEM((kv_len, d), BF16), pltpu.VMEM((kv_len, d), BF16),
                        pltpu.VMEM((grp * tq, d), F32), pltpu.VMEM((grp * tq, d), F32)],
        input_output_aliases=aliases,
        compiler_params=_cparams("parallel", "parallel", "arbitrary"),
        name=name,
    )(*args)


def _router_kernel(x_ref, sh_ref, sc_ref, rwt_ref, rb_ref, hn_ref, idx_ref, wts_ref, hn_s, *, n_sub):
    for g in range(n_sub):
        rows = pl.ds(g * GROUP_ROWS, GROUP_ROWS)
        hn_s[rows, :] = _rms(x_ref[rows, :]) * (1.0 + sc_ref[g:g + 1, :]) + sh_ref[g:g + 1, :]
    hn = hn_s[...]
    _store_token_tiles(hn_ref, hn)

    logits = lax.dot_general(rwt_ref[...], hn, (((1,), (1,)), ((), ())), precision=lax.Precision.HIGHEST,
                             preferred_element_type=F32)
    scores = _sigmoid(logits)
    biased = scores + rb_ref[...]
    n_e, tm = biased.shape
    per = n_e // N_GROUPS
    neg = -jnp.inf
    fsum = lambda x: jnp.sum(x, axis=0, keepdims=True)
    fmax = lambda x: jnp.max(x, axis=0, keepdims=True)

    grp_score = []
    for a in range(N_GROUPS):
        blk = biased[a * per:(a + 1) * per]
        m1 = fmax(blk)
        n_top = fsum((blk == m1).astype(F32))
        m2 = fmax(jnp.where(blk < m1, blk, neg))
        grp_score.append(m1 + jnp.where(n_top > 1.5, m1, m2))
    masked = []
    for a in range(N_GROUPS):
        rank = jnp.zeros_like(grp_score[a])
        for b in range(N_GROUPS):
            if b != a:
                ahead = (grp_score[b] > grp_score[a]) | ((grp_score[b] == grp_score[a]) & (b < a))
                rank = rank + ahead.astype(F32)
        masked.append(jnp.where(rank < TOPK_GROUPS - 0.5, biased[a * per:(a + 1) * per], neg))
    mk = jnp.concatenate(masked, axis=0)
    e_id = lax.broadcasted_iota(I32, (n_e, tm), 0).astype(F32)
    ids, ws = [], []
    for _ in range(TOP_K):
        m = fmax(mk)
        first = jnp.min(jnp.where(mk == m, e_id, float(n_e)), axis=0, keepdims=True)
        hit = e_id == first
        ws.append(fsum(jnp.where(hit, scores, 0.0)))
        ids.append(first)
        mk = jnp.where(hit, neg, mk)
    total = ws[0]
    for w in ws[1:]:
        total = total + w
    idx_ref[...] = jnp.concatenate(ids, axis=0).astype(I32)
    wts_ref[...] = jnp.concatenate([w / total * ROUTED_SCALE for w in ws], axis=0)


def _router(x, sh, sc, rw_t, rb, *, tm):
    n_tok, dm = x.shape
    n_e = rw_t.shape[0]
    n_sub = tm // GROUP_ROWS
    return pl.pallas_call(
        functools.partial(_router_kernel, n_sub=n_sub),
        grid=(n_tok // tm,),
        in_specs=[pl.BlockSpec((tm, dm), lambda i: (i, 0)),
                  pl.BlockSpec((n_sub, dm), lambda i: (i, 0)),
                  pl.BlockSpec((n_sub, dm), lambda i: (i, 0)),
                  pl.BlockSpec((n_e, dm), lambda i: (0, 0)),
                  pl.BlockSpec((n_e, 1), lambda i: (0, 0))],
        out_specs=[pl.BlockSpec((tm * (dm // 2 // LANES), LANES), lambda i: (i, 0)),
                   pl.BlockSpec((TOP_K, tm), lambda i: (0, i)),
                   pl.BlockSpec((TOP_K, tm), lambda i: (0, i))],
        out_shape=[jax.ShapeDtypeStruct((n_tok * (dm // 2 // LANES), LANES), U32),
                   jax.ShapeDtypeStruct((TOP_K, n_tok), I32),
                   jax.ShapeDtypeStruct((TOP_K, n_tok), F32)],
        scratch_shapes=[pltpu.VMEM((tm, dm), F32)],
        compiler_params=_cparams("parallel"),
        name="moe_router",
    )(x, sh, sc, rw_t, rb.reshape(n_e, 1))


IDX_ALIGN = 128
CAST_ROWS = 256
SCATTER_SHIFT = 3
SCATTER_UNROLL = 1 << SCATTER_SHIFT


def _expert_kernel(blk_e_ref, s0_ref, cnt_ref, tok_hbm, slot_hbm, hn_hbm, wgu_ref, wd_ref, *rest, bm, spt, has_prev,
                   run_slot0):
    y_hbm, tbuf, sbuf, xbuf, ybuf, wgu_s, wd_s, gsem, ssem, isem = rest[1:] if has_prev else rest
    i = pl.program_id(0)
    n_blk = pl.num_programs(0)
    cur = i % 2
    win = bm + IDX_ALIGN
    in_order = run_slot0 is not None
    tile = lambda t: pl.ds(pl.multiple_of(t * spt, spt), spt)

    def idx_copies(b):
        a0 = pl.multiple_of(s0_ref[b] & (-IDX_ALIGN), IDX_ALIGN)
        return (pltpu.make_async_copy(tok_hbm.at[pl.ds(a0, win)], tbuf.at[b % 3], isem.at[0, b % 3]),
                pltpu.make_async_copy(slot_hbm.at[pl.ds(a0, win)], sbuf.at[b % 3], isem.at[1, b % 3]))

    def idx_start(b):
        if not in_order:
            for c in idx_copies(b):
                c.start()

    def idx_wait(b):
        if not in_order:
            for c in idx_copies(b):
                c.wait()

    def run_rows(t0):
        return pl.ds(pl.multiple_of(t0 * spt, bm * spt), bm * spt)

    def gather(b, buf):
        if in_order:
            pltpu.make_async_copy(hn_hbm.at[run_rows(s0_ref[b])], xbuf.at[buf], gsem.at[buf]).start()
            return
        buf3 = b % 3
        off = s0_ref[b] & (IDX_ALIGN - 1)
        for r in range(bm):
            pltpu.make_async_copy(hn_hbm.at[tile(tbuf[buf3, off + r])], xbuf.at[buf, pl.ds(r * spt, spt)],
                                  gsem.at[buf]).start(priority=r % 2)

    def wait_gather(buf):
        pltpu.make_async_copy(hn_hbm.at[pl.ds(0, bm * spt)], xbuf.at[buf], gsem.at[buf]).wait()

    def scatter_row(buf3, off, r, queue=0):
        pltpu.make_async_copy(ybuf.at[tile(r)], y_hbm.at[tile(sbuf[buf3, off + r])], ssem.at[0]).start(priority=queue)

    def wait_scatter(n_tok):
        n8 = (n_tok >> SCATTER_SHIFT) << SCATTER_SHIFT
        rows8 = pl.multiple_of(n8 * spt, SCATTER_UNROLL)

        @pl.when(n8 > 0)
        def _():
            pltpu.make_async_copy(ybuf.at[pl.ds(0, rows8)], y_hbm.at[pl.ds(0, rows8)], ssem.at[0]).wait()

        def one_token(r, c):
            pltpu.make_async_copy(ybuf.at[pl.ds(0, spt)], y_hbm.at[pl.ds(0, spt)], ssem.at[0]).wait()
            return c

        lax.fori_loop(n8, n_tok, one_token, 0)

    @pl.when(i == 0)
    def _():
        idx_start(0)
        idx_wait(0)

        @pl.when(cnt_ref[0] > 0)
        def _():
            gather(0, 0)

        @pl.when(n_blk > 1)
        def _():
            idx_start(1)

    @pl.when(i + 1 < n_blk)
    def _():
        idx_wait(i + 1)

        @pl.when(cnt_ref[jnp.minimum(i + 1, n_blk - 1)] > 0)
        def _():
            gather(i + 1, 1 - cur)

    @pl.when(i + 2 < n_blk)
    def _():
        idx_start(i + 2)

    cnt = cnt_ref[i]
    prev_cnt = cnt_ref[jnp.maximum(i - 1, 0)]

    @pl.when((cnt > 0) & ((i == 0) | (blk_e_ref[i] != blk_e_ref[jnp.maximum(i - 1, 0)])))
    def _():
        for dst, src in ((wgu_s, wgu_ref), (wd_s, wd_ref)):
            step = min(CAST_ROWS, dst.shape[0])
            for r0 in range(0, dst.shape[0], step):
                dst[r0:r0 + step, :] = src[0, 0, r0:r0 + step, :].astype(BF16)

    @pl.when(cnt > 0)
    def _():
        wait_gather(cur)
        los, his = _load_token_tiles(xbuf, bm, spt, base=cur)
        half = spt * LANES
        x_lo = jnp.concatenate(los, axis=1).astype(BF16)
        x_hi = jnp.concatenate(his, axis=1).astype(BF16)
        gu = (jnp.dot(x_lo, wgu_s[:half, :], preferred_element_type=F32)
              + jnp.dot(x_hi, wgu_s[half:, :], preferred_element_type=F32))
        de = gu.shape[1] // 2
        h = (_silu(gu[:, :de]) * gu[:, de:]).astype(BF16)
        y = jnp.dot(h, wd_s[...], preferred_element_type=F32)

        @pl.when(i > 0)
        def _():
            wait_scatter(prev_cnt)

        _store_token_tiles(ybuf, y)
        if in_order:
            pltpu.make_async_copy(ybuf, y_hbm.at[run_rows(run_slot0 + s0_ref[i])], ssem.at[0]).start()
        else:
            buf3 = i % 3
            off = s0_ref[i] & (IDX_ALIGN - 1)
            n_grp = cnt >> SCATTER_SHIFT

            def grp_body(j, c):
                for u in range(SCATTER_UNROLL):
                    scatter_row(buf3, off, j * SCATTER_UNROLL + u, queue=u % 2)
                return c

            def row_body(r, c):
                scatter_row(buf3, off, r)
                return c

            lax.fori_loop(0, n_grp, grp_body, 0)
            lax.fori_loop(n_grp * SCATTER_UNROLL, cnt, row_body, 0)

    @pl.when((cnt == 0) & (i > 0))
    def _():
        wait_scatter(prev_cnt)

    @pl.when(i == n_blk - 1)
    def _():
        wait_scatter(cnt)


def _experts(hn_tiles, wgu, wd, layer, plan, y_prev, *, n_slots, bm, name, run_slot0=None):
    blk_e, s0, cnt, tok, slot = plan
    n_blk = blk_e.shape[0]
    _, _, dm, de2 = wgu.shape
    spt = dm // 2 // LANES
    any_spec = pl.BlockSpec(memory_space=pl.ANY)
    in_specs = [any_spec, any_spec, any_spec,
                pl.BlockSpec((1, 1, dm, de2), lambda i, e, s, c: (layer, e[i], 0, 0)),
                pl.BlockSpec((1, 1, de2 // 2, dm), lambda i, e, s, c: (layer, e[i], 0, 0))]
    args = [blk_e, s0, cnt, tok, slot, hn_tiles, wgu, wd]
    aliases = {}
    if y_prev is not None:
        in_specs.append(any_spec)
        args.append(y_prev)
        aliases = {len(args) - 1: 0}
    grid_spec = pltpu.PrefetchScalarGridSpec(
        num_scalar_prefetch=3,
        grid=(n_blk,),
        in_specs=in_specs,
        out_specs=any_spec,
        scratch_shapes=[pltpu.SMEM((3, bm + IDX_ALIGN), I32), pltpu.SMEM((3, bm + IDX_ALIGN), I32),
                        pltpu.VMEM((2, bm * spt, LANES), U32), pltpu.VMEM((bm * spt, LANES), U32),
                        pltpu.VMEM((dm, de2), BF16), pltpu.VMEM((de2 // 2, dm), BF16),
                        pltpu.SemaphoreType.DMA((2,)), pltpu.SemaphoreType.DMA((1,)), pltpu.SemaphoreType.DMA((2, 3))],
    )
    return pl.pallas_call(
        functools.partial(_expert_kernel, bm=bm, spt=spt, has_prev=y_prev is not None, run_slot0=run_slot0),
        grid_spec=grid_spec,
        out_shape=jax.ShapeDtypeStruct((n_slots * spt, LANES), U32),
        input_output_aliases=aliases,
        compiler_params=_cparams("arbitrary"),
        name=name,
    )(*args)


def _combine_kernel(x_ref, y_ref, w_ref, gate_ref, fn_ref, *rest, n_sub, final, n_first):
    outs, ff_s = rest[:-1], rest[-1]
    n_k = y_ref.shape[0]
    tm, dm = ff_s.shape
    half = dm // 2
    spt = half // LANES
    w = w_ref[...]
    wk = [jnp.broadcast_to(w[:, k:k + 1], (tm, LANES)) for k in range(n_k - 1)]
    for s in range(spt):
        piece = lambda k: _unpack_pair(y_ref[k, pl.ds(s, tm, stride=spt), :])
        lo, hi = piece(n_k - 1)
        for k in range(n_k - 1):
            yl, yh = piece(k)
            lo = lo + wk[k] * yl
            hi = hi + wk[k] * yh
        ff_s[:, s * LANES:(s + 1) * LANES] = lo
        ff_s[:, half + s * LANES:half + (s + 1) * LANES] = hi
    for g in range(n_sub):
        rows = pl.ds(g * GROUP_ROWS, GROUP_ROWS)
        out = x_ref[rows, :] + gate_ref[g:g + 1, :] * ff_s[rows, :]
        if final:
            out = _rms(out) * fn_ref[...]
        ff_s[rows, :] = out
    if n_first is None:
        outs[0][...] = ff_s[...]
    else:
        i = pl.program_id(0)

        @pl.when(i < n_first)
        def _():
            outs[0][...] = ff_s[...]

        @pl.when(i >= n_first)
        def _():
            outs[1][...] = ff_s[...]


def _combine(x, y_slots, wts_tok, gate, final_norm, *, tm, final, n_first_rows=None):
    n_tok, dm = x.shape
    n_k = TOP_K + 1
    n_sub = tm // GROUP_ROWS
    spt = dm // 2 // LANES
    y3 = y_slots.reshape(n_k, n_tok * spt, LANES)
    if n_first_rows is None:
        n_first = None
        out_specs = pl.BlockSpec((tm, dm), lambda i: (i, 0))
        out_shape = jax.ShapeDtypeStruct((n_tok, dm), F32)
    else:
        assert n_first_rows % tm == 0 and n_tok - n_first_rows == tm
        n_first = n_first_rows // tm
        out_specs = [pl.BlockSpec((tm, dm), lambda i: (jnp.minimum(i, n_first - 1), 0)),
                     pl.BlockSpec((tm, dm), lambda i: (0, 0))]
        out_shape = [jax.ShapeDtypeStruct((n_first_rows, dm), F32), jax.ShapeDtypeStruct((tm, dm), F32)]
    return pl.pallas_call(
        functools.partial(_combine_kernel, n_sub=n_sub, final=final, n_first=n_first),
        grid=(n_tok // tm,),
        in_specs=[pl.BlockSpec((tm, dm), lambda i: (i, 0)),
                  pl.BlockSpec((n_k, tm * spt, LANES), lambda i: (0, i, 0)),
                  pl.BlockSpec((tm, TOP_K), lambda i: (i, 0)),
                  pl.BlockSpec((n_sub, dm), lambda i: (i, 0)),
                  pl.BlockSpec((1, dm), lambda i: (0, 0))],
        out_specs=out_specs,
        out_shape=out_shape,
        scratch_shapes=[pltpu.VMEM((tm, dm), F32)],
        compiler_params=_cparams("arbitrary"),
        name="moe_combine",
    )(x, y3, wts_tok, gate, final_norm.reshape(1, dm))


def _moe_plan(idx_t, bm):
    n_tok = idx_t.shape[1]
    assert n_tok < (1 << 16)
    flat_e = idx_t.reshape(-1)
    a = flat_e.shape[0]
    _, order = lax.sort((flat_e, jnp.arange(a, dtype=I32)), num_keys=1, is_stable=True)
    tail = jnp.zeros((bm + 2 * IDX_ALIGN,), I32)
    tok = jnp.concatenate([order % n_tok, tail])
    slot = jnp.concatenate([order, tail])
    e_ids = jnp.arange(N_EXPERTS, dtype=I32)
    counts = jnp.sum((flat_e[:, None] == e_ids[None, :]).astype(I32), axis=0)
    start = jnp.cumsum(counts) - counts
    nblk_e = (counts + bm - 1) // bm
    blk_end = jnp.cumsum(nblk_e)
    n_blk = -(-(a + N_EXPERTS * (bm - 1)) // bm)
    b = jnp.arange(n_blk, dtype=I32)
    blk_e = jnp.minimum(jnp.sum((b[:, None] >= blk_end[None, :]).astype(I32), axis=1), N_EXPERTS - 1)
    onehot = (blk_e[:, None] == e_ids[None, :]).astype(I32)
    pick = lambda v: jnp.sum(onehot * v[None, :], axis=1)
    j = b - pick(blk_end - nblk_e)
    s0 = jnp.minimum(pick(start) + j * bm, a)
    cnt = jnp.clip(pick(counts) - j * bm, 0, bm)
    return blk_e, s0, cnt, tok, slot


def _shared_plan(n_tok, bm):
    assert n_tok % bm == 0
    n_blk = n_tok // bm
    tail = jnp.zeros((bm + 2 * IDX_ALIGN,), I32)
    tok = jnp.concatenate([jnp.arange(n_tok, dtype=I32), tail])
    slot = jnp.concatenate([TOP_K * n_tok + jnp.arange(n_tok, dtype=I32), tail])
    return (jnp.zeros((n_blk,), I32), jnp.arange(n_blk, dtype=I32) * bm, jnp.full((n_blk,), bm, I32), tok, slot)


def _moe(x, sh, sc, gate, router_w, router_bias, w_gate_up, w_down, ws_gate_up, ws_down, final_norm, *, layer, final,
         tm, n_first_rows=None):
    n_tok, dm = x.shape
    hn_tiles, idx_t, wts_t = _router(x, sh, sc, router_w.T, router_bias, tm=tm)
    n_slots = (TOP_K + 1) * n_tok
    y_slots = _experts(hn_tiles, w_gate_up, w_down, layer, _moe_plan(idx_t, MOE_BM), None, n_slots=n_slots,
                       bm=MOE_BM, name="moe_experts")
    y_slots = _experts(hn_tiles, ws_gate_up[:, None], ws_down[:, None], layer, _shared_plan(n_tok, MOE_SHARED_BM),
                       y_slots, n_slots=n_slots, bm=MOE_SHARED_BM, name="moe_shared_expert", run_slot0=TOP_K * n_tok)
    return _combine(x, y_slots, wts_t.T, gate, final_norm, tm=tm, final=final, n_first_rows=n_first_rows)


def _group_rows(mod, bp, tp, bs, ts):
    return jnp.concatenate([jnp.repeat(mod[:bp], tp // GROUP_ROWS, axis=0),
                            jnp.repeat(mod[bp:bp + bs], ts // GROUP_ROWS, axis=0)], axis=0)


def kernel(x_prompt, x_sample, c_prompt, c_sample, state_delta, state_conv, cache_k, cache_v, ada_w, ada_b, a_w_in, a_conv_w, a_a_log, a_dt_bias, a_o_norm, a_w_out, kv_ada_w, kv_ada_b, w_kv, b_w_q, b_w_out, router_w, router_bias, w_gate_up, w_down, ws_gate_up, ws_down, final_norm):
    bp, tp, dm = x_prompt.shape
    bs, ts, _ = x_sample.shape
    n_a = a_w_in.shape[0]
    depth = ada_w.shape[0]
    assert n_a == 1 and depth == 2, "one DeltaNet layer followed by one attention layer"
    assert ts == GROUP_ROWS and tp % GROUP_ROWS == 0
    n_p, n_s = bp * tp, bs * ts
    n_tok = n_p + n_s
    d = HEAD_DIM
    hv = state_delta.shape[2]
    val_dim = hv * d
    key_dim = val_dim // 2
    conv_dim = 2 * key_dim + val_dim
    past = cache_k.shape[1]
    n_kv = cache_k.shape[2]
    grp = b_w_q.shape[2] // (n_kv * d)
    tm_big = 1280 if n_tok % 1280 == 0 else 256
    tm_tok = 256

    x = jnp.concatenate([x_prompt.reshape(n_p, dm), x_sample.reshape(n_s, dm)], axis=0)
    c_rows = 16
    c_all = jnp.concatenate([c_prompt, c_sample, jnp.zeros((c_rows - bp - bs, dm), F32)], axis=0)
    groups = lambda m: _group_rows(m, bp, tp, bs, ts)

    def ada(w, b, layer=None):
        return _linear(c_all, w, tm=c_rows, tn=1024, out_dtype=F32, silu_in=True, bias=b, layer=layer, name="adaln")

    mod = ada(ada_w, ada_b[0], layer=0)
    sh_m, sc_m, gt_m, sh_f, sc_f, gt_f = [groups(m) for m in jnp.split(mod, 6, axis=-1)]
    w_in = a_w_in[0].astype(BF16)
    n_main = conv_dim + val_dim
    qkvz = _linear(x, w_in, tm=tm_big, tn=1024, out_dtype=F32, norm=True, mod=(sh_m, sc_m), n_cols=n_main,
                   name="gdn_in_proj")
    gate_tn = 128
    assert n_main % gate_tn == 0 and 2 * hv <= gate_tn
    ba = _linear(x, w_in, tm=tm_big, tn=gate_tn, out_dtype=F32, norm=True, mod=(sh_m, sc_m), n_cols=gate_tn,
                 col_blk0=n_main // gate_tn, name="gdn_gate_proj")[:, :2 * hv]
    n_g = hv // 4
    ba_g = ba.reshape(n_tok, 2, n_g, 4).transpose(2, 0, 1, 3).reshape(n_g, n_tok, 8)
    pad4 = jnp.zeros((n_g, 4), F32)
    al_g = jnp.concatenate([pad4, a_a_log[0].reshape(n_g, 4)], axis=1).reshape(n_g, 1, 8)
    dtb_g = jnp.concatenate([pad4, a_dt_bias[0].reshape(n_g, 4)], axis=1).reshape(n_g, 1, 8)
    gdn = functools.partial(_gated_deltanet, qkvz, ba_g, al_g, dtb_g, a_conv_w[0], key_dim=key_dim, val_dim=val_dim)
    o_gdn, delta_p = gdn(jnp.zeros((bp, CONV_W - 1, conv_dim), F32), jnp.zeros((bp, hv, d, d), F32), a_o_norm[0], None,
                         batch=bp, seq=tp, row_off=0, tt_rows=512, chunk=CHUNK, name="gdn_prompt")
    o_gdn, delta_s = gdn(state_conv[0], state_delta[0], a_o_norm[0], o_gdn,
                         batch=bs, seq=ts, row_off=n_p, tt_rows=ts, chunk=min(CHUNK, ts), name="gdn_sample")
    last_rows = lambda r0, t: lax.slice(qkvz, (r0 + t - (CONV_W - 1), 0), (r0 + t, conv_dim))
    conv_p = jnp.stack([last_rows(b * tp, tp) for b in range(bp)])
    conv_s = jnp.stack([last_rows(n_p + b * ts, ts) for b in range(bs)])
    x = _linear(o_gdn, a_w_out[0].astype(BF16), tm=tm_big, tn=512, out_dtype=F32, res=x, gate=gt_m, name="gdn_out_proj")
    x = _moe(x, sh_f, sc_f, gt_f, router_w[0], router_bias[0], w_gate_up, w_down, ws_gate_up, ws_down,
             final_norm, layer=0, final=False, tm=tm_tok)

    mod = ada(ada_w, ada_b[1], layer=1)
    sh_m, sc_m, gt_m, sh_f, sc_f, gt_f = [groups(m) for m in jnp.split(mod, 6, axis=-1)]
    kv_mod = ada(kv_ada_w, kv_ada_b)
    kv_sh, kv_sc = [groups(m) for m in jnp.split(kv_mod, 2, axis=-1)]
    kv = _linear(x, w_kv.astype(BF16), tm=tm_big, tn=1024, out_dtype=F32, norm=True, mod=(kv_sh, kv_sc), name="kv_proj")
    q = _linear(x, b_w_q[0].astype(BF16), tm=tm_big, tn=1024, out_dtype=BF16, norm=True, mod=(sh_m, sc_m), name="q_proj")
    kd = n_kv * d
    k_p = kv[:n_p, :kd].reshape(bp, tp, n_kv, d)
    v_p = kv[:n_p, kd:].reshape(bp, tp, n_kv, d)
    k_s = kv[n_p:, :kd].reshape(bs, ts, n_kv, d)
    v_s = kv[n_p:, kd:].reshape(bs, ts, n_kv, d)
    kv_len_s = -(-(past + ts) // SB_KEY_TILE) * SB_KEY_TILE
    tail = jnp.zeros((bs, kv_len_s - past - ts, kd), F32)
    k_all = jnp.concatenate([cache_k.reshape(bs, past, kd), k_s.reshape(bs, ts, kd), tail], axis=1).reshape(-1, kd)
    v_all = jnp.concatenate([cache_v.reshape(bs, past, kd), v_s.reshape(bs, ts, kd), tail], axis=1).reshape(-1, kd)
    o_att = _stick_breaking(q, kv, kv, None, batch=bp, seq=tp, q_row_off=0, kv_len=tp, kv_row_off=0, k_col0=0,
                            v_col0=n_kv, n_kv=n_kv, grp=grp, q_start=0, tq=256, name="attn_prompt")
    o_att = _stick_breaking(q, k_all, v_all, o_att, batch=bs, seq=ts, q_row_off=n_p, kv_len=kv_len_s, kv_row_off=0,
                            k_col0=0, v_col0=0, n_kv=n_kv, grp=grp, q_start=past, tq=ts, name="attn_sample")
    x = _linear(o_att, b_w_out[0].astype(BF16), tm=tm_big, tn=1024, out_dtype=F32, res=x, gate=gt_m, name="attn_out_proj")
    assert n_s == tm_tok
    y_p, y_s = _moe(x, sh_f, sc_f, gt_f, router_w[1], router_bias[1], w_gate_up, w_down, ws_gate_up, ws_down,
                    final_norm, layer=1, final=True, tm=tm_tok, n_first_rows=n_p)
    y_p = y_p.reshape(bp, tp, dm)
    y_s = y_s.reshape(bs, ts, dm)
    return (y_p, y_s, k_p, v_p, k_s, v_s, delta_p[None], conv_p[None], delta_s[None], conv_s[None])
```

```python
import functools

import jax
import jax.numpy as jnp
from jax import lax
from jax.experimental import pallas as pl
from jax.experimental.pallas import tpu as pltpu

F32, BF16, I32, U32 = jnp.float32, jnp.bfloat16, jnp.int32, jnp.uint32

NORM_EPS = 1e-6
CHUNK = 64
CONV_W = 4
N_EXPERTS = 64
TOP_K = 8
N_GROUPS = 8
TOPK_GROUPS = 4
ROUTED_SCALE = 2.5
HEAD_DIM = 128
LANES = 128
GROUP_ROWS = 32
MOE_BM = 512
MOE_SHARED_BM = 256
SB_KEY_TILE = 256
VMEM_LIMIT_BYTES = 56 * 1024 * 1024


def _cparams(*sem):
    return pltpu.CompilerParams(dimension_semantics=sem, vmem_limit_bytes=VMEM_LIMIT_BYTES)


def _sigmoid(x):
    return 1.0 / (1.0 + jnp.exp(-x))


def _silu(x):
    return x * _sigmoid(x)


def _softplus(x):
    return jnp.maximum(x, 0.0) + jnp.log(1.0 + jnp.exp(-jnp.abs(x)))


def _rms(x):
    return x * lax.rsqrt(jnp.mean(x * x, axis=-1, keepdims=True) + NORM_EPS)


def _pack_pair(lo, hi):
    lo_bits = lax.bitcast_convert_type(lo.astype(BF16).astype(F32), U32) >> 16
    hi_bits = lax.bitcast_convert_type(hi.astype(BF16).astype(F32), U32) & jnp.uint32(0xFFFF0000)
    return lo_bits | hi_bits


def _unpack_pair(w):
    lo = lax.bitcast_convert_type(w << 16, F32)
    hi = lax.bitcast_convert_type(w & jnp.uint32(0xFFFF0000), F32)
    return lo, hi


def _store_token_tiles(ref, x, base=None):
    n, dm = x.shape
    half = dm // 2
    spt = half // LANES
    words = _pack_pair(x[:, :half], x[:, half:])
    for s in range(spt):
        piece = words[:, s * LANES:(s + 1) * LANES]
        if base is None:
            ref[pl.ds(s, n, stride=spt), :] = piece
        else:
            ref[base, pl.ds(s, n, stride=spt), :] = piece


def _load_token_tiles(ref, n, spt, base=None):
    los, his = [], []
    for s in range(spt):
        w = ref[pl.ds(s, n, stride=spt), :] if base is None else ref[base, pl.ds(s, n, stride=spt), :]
        lo, hi = _unpack_pair(w)
        los.append(lo)
        his.append(hi)
    return los, his


def _linear_kernel(*refs, norm, silu_in, has_mod, has_bias, has_res, prologue, n_sub):
    it = iter(refs)
    x_ref, w_ref = next(it), next(it)
    sh_ref = sc_ref = b_ref = res_ref = gate_ref = xs_ref = None
    if has_mod:
        sh_ref, sc_ref = next(it), next(it)
    if has_bias:
        b_ref = next(it)
    if has_res:
        res_ref, gate_ref = next(it), next(it)
    o_ref = next(it)
    if prologue:
        xs_ref = next(it)

        @pl.when(pl.program_id(1) == 0)
        def _():
            def prep(x):
                x = x.astype(F32)
                if silu_in:
                    x = _silu(x)
                if norm:
                    x = _rms(x)
                return x

            if has_mod:
                for g in range(n_sub):
                    rows = pl.ds(g * GROUP_ROWS, GROUP_ROWS)
                    xg = prep(x_ref[rows, :])
                    xg = xg * (1.0 + sc_ref[g:g + 1, :]) + sh_ref[g:g + 1, :]
                    xs_ref[rows, :] = xg.astype(BF16)
            else:
                xs_ref[...] = prep(x_ref[...]).astype(BF16)

        xs = xs_ref[...]
    else:
        xs = x_ref[...]
    acc = jnp.dot(xs, w_ref[...].astype(BF16), preferred_element_type=F32)
    if has_bias:
        acc = acc + b_ref[...]
    if has_res:
        for g in range(n_sub):
            rows = pl.ds(g * GROUP_ROWS, GROUP_ROWS)
            part = acc[g * GROUP_ROWS:(g + 1) * GROUP_ROWS, :] * gate_ref[g:g + 1, :]
            o_ref[rows, :] = (res_ref[rows, :] + part).astype(o_ref.dtype)
    else:
        o_ref[...] = acc.astype(o_ref.dtype)


def _linear(x, w, *, tm, tn, out_dtype, norm=False, silu_in=False, mod=None, bias=None, res=None, gate=None,
            n_cols=None, col_blk0=0, layer=None, name="linear"):
    m, k = x.shape
    n = w.shape[-1] if n_cols is None else n_cols
    assert m % tm == 0 and n % tn == 0, (m, tm, n, tn)
    has_mod, has_bias, has_res = mod is not None, bias is not None, res is not None
    prologue = norm or silu_in or has_mod or x.dtype != BF16
    n_sub = tm // GROUP_ROWS if (has_mod or has_res) else 0
    if layer is None:
        w_spec = pl.BlockSpec((k, tn), lambda i, j: (0, col_blk0 + j))
    else:
        w_spec = pl.BlockSpec((None, k, tn), lambda i, j: (layer, 0, col_blk0 + j))
    in_specs = [pl.BlockSpec((tm, k), lambda i, j: (i, 0)), w_spec]
    args = [x, w]
    if has_mod:
        in_specs += [pl.BlockSpec((n_sub, k), lambda i, j: (i, 0))] * 2
        args += list(mod)
    if has_bias:
        in_specs.append(pl.BlockSpec((1, tn), lambda i, j: (0, j)))
        args.append(bias.reshape(1, n))
    if has_res:
        in_specs += [pl.BlockSpec((tm, tn), lambda i, j: (i, j)), pl.BlockSpec((n_sub, tn), lambda i, j: (i, j))]
        args += [res, gate]
    return pl.pallas_call(
        functools.partial(_linear_kernel, norm=norm, silu_in=silu_in, has_mod=has_mod, has_bias=has_bias,
                          has_res=has_res, prologue=prologue, n_sub=n_sub),
        grid=(m // tm, n // tn),
        in_specs=in_specs,
        out_specs=pl.BlockSpec((tm, tn), lambda i, j: (i, j)),
        out_shape=jax.ShapeDtypeStruct((m, n), out_dtype),
        scratch_shapes=[pltpu.VMEM((tm, k), BF16)] if prologue else [],
        compiler_params=_cparams("parallel", "arbitrary"),
        name=name,
    )(*args)


def _split_bf16(a):
    hi = a.astype(BF16)
    lo = (a - hi.astype(F32)).astype(BF16)
    return hi, lo


def _bmm(a, b):
    return jnp.einsum("bij,bjk->bik", a, b, preferred_element_type=F32)


def _bmm3(a, b):
    ah, al = _split_bf16(a)
    bh, bl = _split_bf16(b)
    m = a.shape[1]
    both = _bmm(jnp.concatenate([ah, al], axis=1), bh)
    return both[:, :m] + both[:, m:] + _bmm(ah, bl)


def _bmm1(a, b):
    return _bmm(a.astype(BF16), b.astype(BF16))


def _unit_lower_inverse(a, c):
    r = lax.broadcasted_iota(I32, (c, c), 0)
    q = lax.broadcasted_iota(I32, (c, c), 1)
    same = lambda s: (r ^ q) < s
    eye = (r == q).astype(F32)
    a0 = jnp.where(same(8), a, 0.0)
    a2 = _bmm1(a0, a0)
    a4 = _bmm1(a2, a2)
    t = eye - a0
    t = t + _bmm1(t, a2)
    t = t + _bmm1(t, a4)
    s = 8
    while s < c:
        a_off = jnp.where(same(2 * s) & jnp.logical_not(same(s)), a, 0.0)
        t = t - _bmm1(_bmm1(t, a_off), t)
        s *= 2
    return t


def _unit_lower_solve(a, rhs, c):
    t = _unit_lower_inverse(a, c).astype(BF16)
    x0 = _bmm(t, rhs.astype(BF16))
    resid = rhs - (x0 + _bmm3(a, x0))
    return x0 + _bmm(t, resid.astype(BF16))


def _gdn_kernel(q_ref, k_ref, v_ref, z_ref, ba_ref, al_ref, dtb_ref, cwq_ref, cwk_ref, cwv_ref,
                c0q_ref, c0k_ref, c0v_ref, s0_ref, on_ref, o_ref, sfin_ref,
                xq_s, xk_s, xv_s, state_s, *, tt_rows, chunk):
    tt = pl.program_id(2)
    n_tt = pl.num_programs(2)
    d = HEAD_DIM
    nc = tt_rows // chunk
    halo = CONV_W - 1

    @pl.when(tt == 0)
    def _():
        state_s[...] = s0_ref[0]
        xq_s[8 - halo:8, :] = c0q_ref[0]
        xk_s[8 - halo:8, :] = c0k_ref[0]
        xv_s[8 - halo:8, :] = c0v_ref[0]

    def conv_silu(x_ref, xs, cw_ref):
        xs[8:8 + tt_rows, :] = x_ref[...]
        acc = xs[8 - halo:8 - halo + tt_rows, :] * cw_ref[0:1, :]
        for i in range(1, CONV_W):
            acc = acc + xs[8 - halo + i:8 - halo + i + tt_rows, :] * cw_ref[i:i + 1, :]
        xs[8 - halo:8, :] = xs[8 + tt_rows - halo:8 + tt_rows, :]
        return _silu(acc)

    qc = conv_silu(q_ref, xq_s, cwq_ref)
    kc = conv_silu(k_ref, xk_s, cwk_ref)
    vc = conv_silu(v_ref, xv_s, cwv_ref)

    def l2(x):
        return x * lax.rsqrt(jnp.sum(x * x, axis=-1, keepdims=True) + NORM_EPS)

    qh = [(l2(qc[:, j * d:(j + 1) * d]) * d ** -0.5).reshape(nc, chunk, d) for j in range(2)]
    kh = [l2(kc[:, j * d:(j + 1) * d]).reshape(nc, chunk, d) for j in range(2)]
    vh = [vc[:, h * d:(h + 1) * d].reshape(nc, chunk, d) for h in range(4)]

    def heads4(per_key_head):
        a, b = per_key_head
        st = jnp.stack([a, a, b, b], axis=1)
        return st.reshape((nc * 4,) + st.shape[2:])

    ba = ba_ref[0]
    beta_all = _sigmoid(ba)
    g_all = -jnp.exp(al_ref[0]) * _softplus(ba + dtb_ref[0])
    mm = lambda a, b: jnp.dot(a, b, preferred_element_type=F32)

    def col4(x, lane0):
        cols = [x[:, lane0 + h:lane0 + h + 1].reshape(nc, chunk, 1) for h in range(4)]
        return jnp.stack(cols, axis=1).reshape(nc * 4, chunk, 1)

    beta = col4(beta_all, 0)
    g_col = col4(g_all, 4)
    nb = nc * 4

    r = lax.broadcasted_iota(I32, (chunk, chunk), 0)
    q_i = lax.broadcasted_iota(I32, (chunk, chunk), 1)
    eye, incl, strict = (r == q_i), (r >= q_i), (r > q_i)
    g_b = jnp.broadcast_to(g_col, (nb, chunk, chunk))
    g_t = jnp.sum(jnp.where(eye, g_b, 0.0), axis=1, keepdims=True)
    gc_t = jnp.sum(jnp.where(r <= q_i, g_b, 0.0), axis=1, keepdims=True)
    gc = jnp.sum(jnp.where(incl, jnp.broadcast_to(g_t, (nb, chunk, chunk)), 0.0), axis=2, keepdims=True)
    decay = jnp.where(incl, jnp.exp(jnp.where(incl, gc - gc_t, 0.0)), 0.0)

    k2 = jnp.concatenate(kh, axis=0).astype(BF16)
    q2 = jnp.concatenate(qh, axis=0).astype(BF16)
    kk = jnp.einsum("bid,bjd->bij", k2, k2, preferred_element_type=F32)
    qk = jnp.einsum("bid,bjd->bij", q2, k2, preferred_element_type=F32)
    kk4 = heads4((kk[:nc], kk[nc:]))
    qk4 = heads4((qk[:nc], qk[nc:]))
    k4 = heads4(kh)
    q4 = heads4(qh)
    v4 = jnp.stack(vh, axis=1).reshape(nb, chunk, d)

    a_mat = jnp.where(strict, kk4 * decay * beta, 0.0)
    attn = (qk4 * decay).astype(BF16)
    eg = jnp.exp(gc)
    g_last = gc[:, chunk - 1:chunk, :]
    rhs = jnp.concatenate([v4 * beta, k4 * (beta * eg)], axis=-1)
    sol = _unit_lower_solve(a_mat, rhs, chunk)
    u_all, w_all = sol[:, :, :d], sol[:, :, d:].astype(BF16)
    qd_all = (q4 * eg).astype(BF16)
    kt_all = k4 * jnp.exp(g_last - gc)
    egl = jnp.exp(g_last)

    on = on_ref[...]
    s = [state_s[h] for h in range(4)]
    for c in range(nc):
        for h in range(4):
            b = c * 4 + h
            sb = s[h].astype(BF16)
            v_new = u_all[b] - mm(w_all[b], sb)
            o = mm(qd_all[b], sb) + mm(attn[b], v_new.astype(BF16))
            s[h] = s[h] * egl[b] + mm(kt_all[b].T.astype(BF16), v_new.astype(BF16))
            rows = pl.ds(c * chunk, chunk)
            z = z_ref[rows, h * d:(h + 1) * d]
            o_ref[rows, h * d:(h + 1) * d] = (_rms(o) * on * _silu(z)).astype(o_ref.dtype)
    for h in range(4):
        state_s[h] = s[h]

    @pl.when(tt == n_tt - 1)
    def _():
        for h in range(4):
            sfin_ref[0, h] = s[h]


def _gated_deltanet(qkvz, ba_g, al_g, dtb_g, conv_w, conv0, s0, o_norm, o_prev, *, batch, seq, row_off, tt_rows,
                    chunk, key_dim, val_dim, name):
    n_tok = qkvz.shape[0]
    d = HEAD_DIM
    n_g = key_dim // (2 * d)
    assert val_dim == 2 * key_dim and seq % tt_rows == 0 and row_off % tt_rows == 0 and tt_rows % chunk == 0
    n_tt = seq // tt_rows
    rb0 = row_off // tt_rows
    kq, kv_ = key_dim // (2 * d), (2 * key_dim) // (4 * d)

    row = lambda b, g, t: rb0 + b * n_tt + t
    in_specs = [
        pl.BlockSpec((tt_rows, 2 * d), lambda b, g, t: (row(b, g, t), g)),
        pl.BlockSpec((tt_rows, 2 * d), lambda b, g, t: (row(b, g, t), kq + g)),
        pl.BlockSpec((tt_rows, 4 * d), lambda b, g, t: (row(b, g, t), kv_ + g)),
        pl.BlockSpec((tt_rows, 4 * d), lambda b, g, t: (row(b, g, t), 2 * kv_ + g)),
        pl.BlockSpec((1, tt_rows, 8), lambda b, g, t: (g, row(b, g, t), 0)),
        pl.BlockSpec((1, 1, 8), lambda b, g, t: (g, 0, 0)),
        pl.BlockSpec((1, 1, 8), lambda b, g, t: (g, 0, 0)),
        pl.BlockSpec((CONV_W, 2 * d), lambda b, g, t: (0, g)),
        pl.BlockSpec((CONV_W, 2 * d), lambda b, g, t: (0, kq + g)),
        pl.BlockSpec((CONV_W, 4 * d), lambda b, g, t: (0, kv_ + g)),
        pl.BlockSpec((1, CONV_W - 1, 2 * d), lambda b, g, t: (b, 0, g)),
        pl.BlockSpec((1, CONV_W - 1, 2 * d), lambda b, g, t: (b, 0, kq + g)),
        pl.BlockSpec((1, CONV_W - 1, 4 * d), lambda b, g, t: (b, 0, kv_ + g)),
        pl.BlockSpec((1, 4, d, d), lambda b, g, t: (b, g, 0, 0)),
        pl.BlockSpec((1, d), lambda b, g, t: (0, 0)),
    ]
    args = [qkvz, qkvz, qkvz, qkvz, ba_g, al_g, dtb_g, conv_w, conv_w, conv_w, conv0, conv0, conv0, s0,
            o_norm.reshape(1, d)]
    aliases = {}
    if o_prev is not None:
        in_specs.append(pl.BlockSpec(memory_space=pl.ANY))
        args.append(o_prev)
        aliases = {len(args) - 1: 0}
    kern = functools.partial(_gdn_kernel, tt_rows=tt_rows, chunk=chunk)
    if o_prev is not None:
        kern = functools.partial(_drop_arg, kern, len(args) - 1)
    return pl.pallas_call(
        kern,
        grid=(batch, n_g, n_tt),
        in_specs=in_specs,
        out_specs=[pl.BlockSpec((tt_rows, 4 * d), lambda b, g, t: (row(b, g, t), g)),
                   pl.BlockSpec((1, 4, d, d), lambda b, g, t: (b, g, 0, 0))],
        out_shape=[jax.ShapeDtypeStruct((n_tok, val_dim), BF16),
                   jax.ShapeDtypeStruct((batch, 4 * n_g, d, d), F32)],
        scratch_shapes=[pltpu.VMEM((8 + tt_rows, 2 * d), F32), pltpu.VMEM((8 + tt_rows, 2 * d), F32),
                        pltpu.VMEM((8 + tt_rows, 4 * d), F32), pltpu.VMEM((4, d, d), F32)],
        input_output_aliases=aliases,
        compiler_params=_cparams("parallel", "parallel", "arbitrary"),
        name=name,
    )(*args)


def _drop_arg(kern, pos, *refs):
    return kern(*refs[:pos], *refs[pos + 1:])


def _sb_kernel(q_ref, k_ref, v_ref, o_ref, kb_s, vb_s, oacc_s, carry_s, *, tq, tk, q_start, grp):
    qi = pl.program_id(2)
    d = HEAD_DIM
    rows = grp * tq

    @pl.when(qi == 0)
    def _():
        kb_s[...] = k_ref[...].astype(BF16)
        vb_s[...] = v_ref[...].astype(BF16)

    q = q_ref[...]
    qs = jnp.concatenate([q[:, h * d:(h + 1) * d] for h in range(grp)], axis=0)
    oacc_s[...] = jnp.zeros_like(oacc_s)
    carry_s[...] = jnp.zeros_like(carry_s)

    q_lo = q_start + qi * tq
    q_hi = q_lo + tq - 1
    n_kt = (q_hi + tk - 1) // tk
    n_full = q_lo // tk

    u = (lax.broadcasted_iota(I32, (tk, tk), 0) > lax.broadcasted_iota(I32, (tk, tk), 1)).astype(BF16)
    u2 = jnp.concatenate([u, u], axis=0)
    qpos = q_lo + (lax.broadcasted_iota(I32, (rows, tk), 0) & (tq - 1))
    lane = lax.broadcasted_iota(I32, (rows, tk), 1)
    c2 = d ** -0.5 * 1.4426950408889634
    sign = jnp.uint32(0x80000000)

    def tile(kt, masked):
        k0 = pl.multiple_of(kt * tk, tk)
        kblk = kb_s[pl.ds(k0, tk), :]
        vblk = vb_s[pl.ds(k0, tk), :]
        z = lax.dot_general(qs, kblk, (((1,), (1,)), ((), ())), preferred_element_type=F32) * c2
        neg_abs = lax.bitcast_convert_type(lax.bitcast_convert_type(z, U32) | sign, F32)
        t = jnp.maximum(z, 0.0) + jnp.log2(1.0 + jnp.exp2(neg_abs))
        if masked:
            m = (k0 + lane) < qpos
            t = jnp.where(m, t, 0.0)
        hi, lo = _split_bf16(t)
        later = jnp.dot(jnp.concatenate([hi, lo], axis=1), u2, preferred_element_type=F32)
        carry = carry_s[...]
        a = jnp.exp2((z - t) - later - jnp.concatenate([carry] * (tk // d), axis=1))
        if masked:
            a = jnp.where(m, a, 0.0)
        oacc_s[...] += jnp.dot(a.astype(BF16), vblk, preferred_element_type=F32)
        carry_s[...] = carry + jnp.broadcast_to(later[:, 0:1] + t[:, 0:1], carry.shape)

    def masked_body(it, c):
        tile(n_kt - 1 - it, True)
        return c

    def full_body(it, c):
        tile(n_full - 1 - it, False)
        return c

    lax.fori_loop(0, n_kt - n_full, masked_body, 0)
    lax.fori_loop(0, n_full, full_body, 0)
    for h in range(grp):
        o_ref[:, h * d:(h + 1) * d] = oacc_s[h * tq:(h + 1) * tq, :].astype(o_ref.dtype)


def _stick_breaking(q, k_arr, v_arr, o_prev, *, batch, seq, q_row_off, kv_len, kv_row_off, k_col0, v_col0,
                    n_kv, grp, q_start, tq, name):
    n_tok, qd = q.shape
    d = HEAD_DIM
    tk = SB_KEY_TILE
    assert seq % tq == 0 and q_row_off % tq == 0 and kv_row_off % kv_len == 0 and kv_len % tk == 0
    assert tq & (tq - 1) == 0
    assert q_start + seq <= kv_len + tk and (q_start + seq - 1 + tk - 1) // tk * tk <= kv_len
    n_q = seq // tq
    in_specs = [
        pl.BlockSpec((tq, grp * d), lambda b, g, i: (q_row_off // tq + b * n_q + i, g)),
        pl.BlockSpec((kv_len, d), lambda b, g, i: (kv_row_off // kv_len + b, k_col0 + g)),
        pl.BlockSpec((kv_len, d), lambda b, g, i: (kv_row_off // kv_len + b, v_col0 + g)),
    ]
    args = [q, k_arr, v_arr]
    kern = functools.partial(_sb_kernel, tq=tq, tk=tk, q_start=q_start, grp=grp)
    aliases = {}
    if o_prev is not None:
        in_specs.append(pl.BlockSpec(memory_space=pl.ANY))
        args.append(o_prev)
        aliases = {3: 0}
        kern = functools.partial(_drop_arg, kern, 3)
    return pl.pallas_call(
        kern,
        grid=(batch, n_kv, n_q),
        in_specs=in_specs,
        out_specs=pl.BlockSpec((tq, grp * d), lambda b, g, i: (q_row_off // tq + b * n_q + i, g)),
        out_shape=jax.ShapeDtypeStruct((n_tok, qd), BF16),
        scratch_shapes=[pltpu.VMEM((kv_len, d), BF16), pltpu.VMEM((kv_len, d), BF16),
                        pltpu.VMEM((grp * tq, d), F32), pltpu.VMEM((grp * tq, d), F32)],
        input_output_aliases=aliases,
        compiler_params=_cparams("parallel", "parallel", "arbitrary"),
        name=name,
    )(*args)


def _router_kernel(x_ref, sh_ref, sc_ref, rwt_ref, rb_ref, hn_ref, idx_ref, wts_ref, hn_s, *, n_sub):
    for g in range(n_sub):
        rows = pl.ds(g * GROUP_ROWS, GROUP_ROWS)
        hn_s[rows, :] = _rms(x_ref[rows, :]) * (1.0 + sc_ref[g:g + 1, :]) + sh_ref[g:g + 1, :]
    hn = hn_s[...]
    _store_token_tiles(hn_ref, hn)

    logits = lax.dot_general(rwt_ref[...], hn, (((1,), (1,)), ((), ())), precision=lax.Precision.HIGHEST,
                             preferred_element_type=F32)
    scores = _sigmoid(logits)
    biased = scores + rb_ref[...]
    n_e, tm = biased.shape
    per = n_e // N_GROUPS
    neg = -jnp.inf
    fsum = lambda x: jnp.sum(x, axis=0, keepdims=True)
    fmax = lambda x: jnp.max(x, axis=0, keepdims=True)

    grp_score = []
    for a in range(N_GROUPS):
        blk = biased[a * per:(a + 1) * per]
        m1 = fmax(blk)
        n_top = fsum((blk == m1).astype(F32))
        m2 = fmax(jnp.where(blk < m1, blk, neg))
        grp_score.append(m1 + jnp.where(n_top > 1.5, m1, m2))
    masked = []
    for a in range(N_GROUPS):
        rank = jnp.zeros_like(grp_score[a])
        for b in range(N_GROUPS):
            if b != a:
                ahead = (grp_score[b] > grp_score[a]) | ((grp_score[b] == grp_score[a]) & (b < a))
                rank = rank + ahead.astype(F32)
        masked.append(jnp.where(rank < TOPK_GROUPS - 0.5, biased[a * per:(a + 1) * per], neg))
    mk = jnp.concatenate(masked, axis=0)
    e_id = lax.broadcasted_iota(I32, (n_e, tm), 0).astype(F32)
    ids, ws = [], []
    for _ in range(TOP_K):
        m = fmax(mk)
        first = jnp.min(jnp.where(mk == m, e_id, float(n_e)), axis=0, keepdims=True)
        hit = e_id == first
        ws.append(fsum(jnp.where(hit, scores, 0.0)))
        ids.append(first)
        mk = jnp.where(hit, neg, mk)
    total = ws[0]
    for w in ws[1:]:
        total = total + w
    idx_ref[...] = jnp.concatenate(ids, axis=0).astype(I32)
    wts_ref[...] = jnp.concatenate([w / total * ROUTED_SCALE for w in ws], axis=0)


def _router(x, sh, sc, rw_t, rb, *, tm):
    n_tok, dm = x.shape
    n_e = rw_t.shape[0]
    n_sub = tm // GROUP_ROWS
    return pl.pallas_call(
        functools.partial(_router_kernel, n_sub=n_sub),
        grid=(n_tok // tm,),
        in_specs=[pl.BlockSpec((tm, dm), lambda i: (i, 0)),
                  pl.BlockSpec((n_sub, dm), lambda i: (i, 0)),
                  pl.BlockSpec((n_sub, dm), lambda i: (i, 0)),
                  pl.BlockSpec((n_e, dm), lambda i: (0, 0)),
                  pl.BlockSpec((n_e, 1), lambda i: (0, 0))],
        out_specs=[pl.BlockSpec((tm * (dm // 2 // LANES), LANES), lambda i: (i, 0)),
                   pl.BlockSpec((TOP_K, tm), lambda i: (0, i)),
                   pl.BlockSpec((TOP_K, tm), lambda i: (0, i))],
        out_shape=[jax.ShapeDtypeStruct((n_tok * (dm // 2 // LANES), LANES), U32),
                   jax.ShapeDtypeStruct((TOP_K, n_tok), I32),
                   jax.ShapeDtypeStruct((TOP_K, n_tok), F32)],
        scratch_shapes=[pltpu.VMEM((tm, dm), F32)],
        compiler_params=_cparams("parallel"),
        name="moe_router",
    )(x, sh, sc, rw_t, rb.reshape(n_e, 1))


IDX_ALIGN = 128
CAST_ROWS = 256
SCATTER_SHIFT = 3
SCATTER_UNROLL = 1 << SCATTER_SHIFT


def _expert_kernel(blk_e_ref, s0_ref, cnt_ref, tok_hbm, slot_hbm, hn_hbm, wgu_ref, wd_ref, *rest, bm, spt, has_prev,
                   run_slot0):
    y_hbm, tbuf, sbuf, xbuf, ybuf, wgu_s, wd_s, gsem, ssem, isem = rest[1:] if has_prev else rest
    i = pl.program_id(0)
    n_blk = pl.num_programs(0)
    cur = i % 2
    win = bm + IDX_ALIGN
    in_order = run_slot0 is not None
    tile = lambda t: pl.ds(pl.multiple_of(t * spt, spt), spt)

    def idx_copies(b):
        a0 = pl.multiple_of(s0_ref[b] & (-IDX_ALIGN), IDX_ALIGN)
        return (pltpu.make_async_copy(tok_hbm.at[pl.ds(a0, win)], tbuf.at[b % 3], isem.at[0, b % 3]),
                pltpu.make_async_copy(slot_hbm.at[pl.ds(a0, win)], sbuf.at[b % 3], isem.at[1, b % 3]))

    def idx_start(b):
        if not in_order:
            for c in idx_copies(b):
                c.start()

    def idx_wait(b):
        if not in_order:
            for c in idx_copies(b):
                c.wait()

    def run_rows(t0):
        return pl.ds(pl.multiple_of(t0 * spt, bm * spt), bm * spt)

    def gather(b, buf):
        if in_order:
            pltpu.make_async_copy(hn_hbm.at[run_rows(s0_ref[b])], xbuf.at[buf], gsem.at[buf]).start()
            return
        buf3 = b % 3
        off = s0_ref[b] & (IDX_ALIGN - 1)
        for r in range(bm):
            pltpu.make_async_copy(hn_hbm.at[tile(tbuf[buf3, off + r])], xbuf.at[buf, pl.ds(r * spt, spt)],
                                  gsem.at[buf]).start(priority=r % 2)

    def wait_gather(buf):
        pltpu.make_async_copy(hn_hbm.at[pl.ds(0, bm * spt)], xbuf.at[buf], gsem.at[buf]).wait()

    def scatter_row(buf3, off, r, queue=0):
        pltpu.make_async_copy(ybuf.at[tile(r)], y_hbm.at[tile(sbuf[buf3, off + r])], ssem.at[0]).start(priority=queue)

    def wait_scatter(n_tok):
        n8 = (n_tok >> SCATTER_SHIFT) << SCATTER_SHIFT
        rows8 = pl.multiple_of(n8 * spt, SCATTER_UNROLL)

        @pl.when(n8 > 0)
        def _():
            pltpu.make_async_copy(ybuf.at[pl.ds(0, rows8)], y_hbm.at[pl.ds(0, rows8)], ssem.at[0]).wait()

        def one_token(r, c):
            pltpu.make_async_copy(ybuf.at[pl.ds(0, spt)], y_hbm.at[pl.ds(0, spt)], ssem.at[0]).wait()
            return c

        lax.fori_loop(n8, n_tok, one_token, 0)

    @pl.when(i == 0)
    def _():
        idx_start(0)
        idx_wait(0)

        @pl.when(cnt_ref[0] > 0)
        def _():
            gather(0, 0)

        @pl.when(n_blk > 1)
        def _():
            idx_start(1)

    @pl.when(i + 1 < n_blk)
    def _():
        idx_wait(i + 1)

        @pl.when(cnt_ref[jnp.minimum(i + 1, n_blk - 1)] > 0)
        def _():
            gather(i + 1, 1 - cur)

    @pl.when(i + 2 < n_blk)
    def _():
        idx_start(i + 2)

    cnt = cnt_ref[i]
    prev_cnt = cnt_ref[jnp.maximum(i - 1, 0)]

    @pl.when((cnt > 0) & ((i == 0) | (blk_e_ref[i] != blk_e_ref[jnp.maximum(i - 1, 0)])))
    def _():
        for dst, src in ((wgu_s, wgu_ref), (wd_s, wd_ref)):
            step = min(CAST_ROWS, dst.shape[0])
            for r0 in range(0, dst.shape[0], step):
                dst[r0:r0 + step, :] = src[0, 0, r0:r0 + step, :].astype(BF16)

    @pl.when(cnt > 0)
    def _():
        wait_gather(cur)
        los, his = _load_token_tiles(xbuf, bm, spt, base=cur)
        half = spt * LANES
        x_lo = jnp.concatenate(los, axis=1).astype(BF16)
        x_hi = jnp.concatenate(his, axis=1).astype(BF16)
        gu = (jnp.dot(x_lo, wgu_s[:half, :], preferred_element_type=F32)
              + jnp.dot(x_hi, wgu_s[half:, :], preferred_element_type=F32))
        de = gu.shape[1] // 2
        h = (_silu(gu[:, :de]) * gu[:, de:]).astype(BF16)
        y = jnp.dot(h, wd_s[...], preferred_element_type=F32)

        @pl.when(i > 0)
        def _():
            wait_scatter(prev_cnt)

        _store_token_tiles(ybuf, y)
        if in_order:
            pltpu.make_async_copy(ybuf, y_hbm.at[run_rows(run_slot0 + s0_ref[i])], ssem.at[0]).start()
        else:
            buf3 = i % 3
            off = s0_ref[i] & (IDX_ALIGN - 1)
            n_grp = cnt >> SCATTER_SHIFT

            def grp_body(j, c):
                for u in range(SCATTER_UNROLL):
                    scatter_row(buf3, off, j * SCATTER_UNROLL + u, queue=u % 2)
                return c

            def row_body(r, c):
                scatter_row(buf3, off, r)
                return c

            lax.fori_loop(0, n_grp, grp_body, 0)
            lax.fori_loop(n_grp * SCATTER_UNROLL, cnt, row_body, 0)

    @pl.when((cnt == 0) & (i > 0))
    def _():
        wait_scatter(prev_cnt)

    @pl.when(i == n_blk - 1)
    def _():
        wait_scatter(cnt)


def _experts(hn_tiles, wgu, wd, layer, plan, y_prev, *, n_slots, bm, name, run_slot0=None):
    blk_e, s0, cnt, tok, slot = plan
    n_blk = blk_e.shape[0]
    _, _, dm, de2 = wgu.shape
    spt = dm // 2 // LANES
    any_spec = pl.BlockSpec(memory_space=pl.ANY)
    in_specs = [any_spec, any_spec, any_spec,
                pl.BlockSpec((1, 1, dm, de2), lambda i, e, s, c: (layer, e[i], 0, 0)),
                pl.BlockSpec((1, 1, de2 // 2, dm), lambda i, e, s, c: (layer, e[i], 0, 0))]
    args = [blk_e, s0, cnt, tok, slot, hn_tiles, wgu, wd]
    aliases = {}
    if y_prev is not None:
        in_specs.append(any_spec)
        args.append(y_prev)
        aliases = {len(args) - 1: 0}
    grid_spec = pltpu.PrefetchScalarGridSpec(
        num_scalar_prefetch=3,
        grid=(n_blk,),
        in_specs=in_specs,
        out_specs=any_spec,
        scratch_shapes=[pltpu.SMEM((3, bm + IDX_ALIGN), I32), pltpu.SMEM((3, bm + IDX_ALIGN), I32),
                        pltpu.VMEM((2, bm * spt, LANES), U32), pltpu.VMEM((bm * spt, LANES), U32),
                        pltpu.VMEM((dm, de2), BF16), pltpu.VMEM((de2 // 2, dm), BF16),
                        pltpu.SemaphoreType.DMA((2,)), pltpu.SemaphoreType.DMA((1,)), pltpu.SemaphoreType.DMA((2, 3))],
    )
    return pl.pallas_call(
        functools.partial(_expert_kernel, bm=bm, spt=spt, has_prev=y_prev is not None, run_slot0=run_slot0),
        grid_spec=grid_spec,
        out_shape=jax.ShapeDtypeStruct((n_slots * spt, LANES), U32),
        input_output_aliases=aliases,
        compiler_params=_cparams("arbitrary"),
        name=name,
    )(*args)


def _combine_kernel(x_ref, y_ref, w_ref, gate_ref, fn_ref, *rest, n_sub, final, n_first):
    outs, ff_s = rest[:-1], rest[-1]
    n_k = y_ref.shape[0]
    tm, dm = ff_s.shape
    half = dm // 2
    spt = half // LANES
    w = w_ref[...]
    wk = [jnp.broadcast_to(w[:, k:k + 1], (tm, LANES)) for k in range(n_k - 1)]
    for s in range(spt):
        piece = lambda k: _unpack_pair(y_ref[k, pl.ds(s, tm, stride=spt), :])
        lo, hi = piece(n_k - 1)
        for k in range(n_k - 1):
            yl, yh = piece(k)
            lo = lo + wk[k] * yl
            hi = hi + wk[k] * yh
        ff_s[:, s * LANES:(s + 1) * LANES] = lo
        ff_s[:, half + s * LANES:half + (s + 1) * LANES] = hi
    for g in range(n_sub):
        rows = pl.ds(g * GROUP_ROWS, GROUP_ROWS)
        out = x_ref[rows, :] + gate_ref[g:g + 1, :] * ff_s[rows, :]
        if final:
            out = _rms(out) * fn_ref[...]
        ff_s[rows, :] = out
    if n_first is None:
        outs[0][...] = ff_s[...]
    else:
        i = pl.program_id(0)

        @pl.when(i < n_first)
        def _():
            outs[0][...] = ff_s[...]

        @pl.when(i >= n_first)
        def _():
            outs[1][...] = ff_s[...]


def _combine(x, y_slots, wts_tok, gate, final_norm, *, tm, final, n_first_rows=None):
    n_tok, dm = x.shape
    n_k = TOP_K + 1
    n_sub = tm // GROUP_ROWS
    spt = dm // 2 // LANES
    y3 = y_slots.reshape(n_k, n_tok * spt, LANES)
    if n_first_rows is None:
        n_first = None
        out_specs = pl.BlockSpec((tm, dm), lambda i: (i, 0))
        out_shape = jax.ShapeDtypeStruct((n_tok, dm), F32)
    else:
        assert n_first_rows % tm == 0 and n_tok - n_first_rows == tm
        n_first = n_first_rows // tm
        out_specs = [pl.BlockSpec((tm, dm), lambda i: (jnp.minimum(i, n_first - 1), 0)),
                     pl.BlockSpec((tm, dm), lambda i: (0, 0))]
        out_shape = [jax.ShapeDtypeStruct((n_first_rows, dm), F32), jax.ShapeDtypeStruct((tm, dm), F32)]
    return pl.pallas_call(
        functools.partial(_combine_kernel, n_sub=n_sub, final=final, n_first=n_first),
        grid=(n_tok // tm,),
        in_specs=[pl.BlockSpec((tm, dm), lambda i: (i, 0)),
                  pl.BlockSpec((n_k, tm * spt, LANES), lambda i: (0, i, 0)),
                  pl.BlockSpec((tm, TOP_K), lambda i: (i, 0)),
                  pl.BlockSpec((n_sub, dm), lambda i: (i, 0)),
                  pl.BlockSpec((1, dm), lambda i: (0, 0))],
        out_specs=out_specs,
        out_shape=out_shape,
        scratch_shapes=[pltpu.VMEM((tm, dm), F32)],
        compiler_params=_cparams("arbitrary"),
        name="moe_combine",
    )(x, y3, wts_tok, gate, final_norm.reshape(1, dm))


def _moe_plan(idx_t, bm):
    n_tok = idx_t.shape[1]
    assert n_tok < (1 << 16)
    flat_e = idx_t.reshape(-1)
    a = flat_e.shape[0]
    id_bits = max(1, (a - 1).bit_length())
    assert N_EXPERTS << id_bits < (1 << 31)
    order = lax.sort((flat_e << id_bits) | jnp.arange(a, dtype=I32)) & ((1 << id_bits) - 1)
    tail = jnp.zeros((bm + 2 * IDX_ALIGN,), I32)
    tok = jnp.concatenate([order % n_tok, tail])
    slot = jnp.concatenate([order, tail])
    e_ids = jnp.arange(N_EXPERTS, dtype=I32)
    counts = jnp.sum((flat_e[:, None] == e_ids[None, :]).astype(I32), axis=0)
    start = jnp.cumsum(counts) - counts
    nblk_e = (counts + bm - 1) // bm
    blk_end = jnp.cumsum(nblk_e)
    n_blk = -(-(a + N_EXPERTS * (bm - 1)) // bm)
    b = jnp.arange(n_blk, dtype=I32)
    blk_e = jnp.minimum(jnp.sum((b[:, None] >= blk_end[None, :]).astype(I32), axis=1), N_EXPERTS - 1)
    onehot = (blk_e[:, None] == e_ids[None, :]).astype(I32)
    pick = lambda v: jnp.sum(onehot * v[None, :], axis=1)
    j = b - pick(blk_end - nblk_e)
    s0 = jnp.minimum(pick(start) + j * bm, a)
    cnt = jnp.clip(pick(counts) - j * bm, 0, bm)
    return blk_e, s0, cnt, tok, slot


def _shared_plan(n_tok, bm):
    assert n_tok % bm == 0
    n_blk = n_tok // bm
    tail = jnp.zeros((bm + 2 * IDX_ALIGN,), I32)
    tok = jnp.concatenate([jnp.arange(n_tok, dtype=I32), tail])
    slot = jnp.concatenate([TOP_K * n_tok + jnp.arange(n_tok, dtype=I32), tail])
    return (jnp.zeros((n_blk,), I32), jnp.arange(n_blk, dtype=I32) * bm, jnp.full((n_blk,), bm, I32), tok, slot)


def _moe(x, sh, sc, gate, router_w, router_bias, w_gate_up, w_down, ws_gate_up, ws_down, final_norm, *, layer, final,
         tm, n_first_rows=None):
    n_tok, dm = x.shape
    hn_tiles, idx_t, wts_t = _router(x, sh, sc, router_w.T, router_bias, tm=tm)
    n_slots = (TOP_K + 1) * n_tok
    y_slots = _experts(hn_tiles, w_gate_up, w_down, layer, _moe_plan(idx_t, MOE_BM), None, n_slots=n_slots,
                       bm=MOE_BM, name="moe_experts")
    y_slots = _experts(hn_tiles, ws_gate_up[:, None], ws_down[:, None], layer, _shared_plan(n_tok, MOE_SHARED_BM),
                       y_slots, n_slots=n_slots, bm=MOE_SHARED_BM, name="moe_shared_expert", run_slot0=TOP_K * n_tok)
    return _combine(x, y_slots, wts_t.T, gate, final_norm, tm=tm, final=final, n_first_rows=n_first_rows)


def _group_rows(mod, bp, tp, bs, ts):
    return jnp.concatenate([jnp.repeat(mod[:bp], tp // GROUP_ROWS, axis=0),
                            jnp.repeat(mod[bp:bp + bs], ts // GROUP_ROWS, axis=0)], axis=0)


def kernel(x_prompt, x_sample, c_prompt, c_sample, state_delta, state_conv, cache_k, cache_v, ada_w, ada_b, a_w_in, a_conv_w, a_a_log, a_dt_bias, a_o_norm, a_w_out, kv_ada_w, kv_ada_b, w_kv, b_w_q, b_w_out, router_w, router_bias, w_gate_up, w_down, ws_gate_up, ws_down, final_norm):
    bp, tp, dm = x_prompt.shape
    bs, ts, _ = x_sample.shape
    n_a = a_w_in.shape[0]
    depth = ada_w.shape[0]
    assert n_a == 1 and depth == 2, "one DeltaNet layer followed by one attention layer"
    assert ts == GROUP_ROWS and tp % GROUP_ROWS == 0
    n_p, n_s = bp * tp, bs * ts
    n_tok = n_p + n_s
    d = HEAD_DIM
    hv = state_delta.shape[2]
    val_dim = hv * d
    key_dim = val_dim // 2
    conv_dim = 2 * key_dim + val_dim
    past = cache_k.shape[1]
    n_kv = cache_k.shape[2]
    grp = b_w_q.shape[2] // (n_kv * d)
    tm_big = 1280 if n_tok % 1280 == 0 else 256
    tm_tok = 256

    x = jnp.concatenate([x_prompt.reshape(n_p, dm), x_sample.reshape(n_s, dm)], axis=0)
    c_rows = 16
    c_all = jnp.concatenate([c_prompt, c_sample, jnp.zeros((c_rows - bp - bs, dm), F32)], axis=0)
    groups = lambda m: _group_rows(m, bp, tp, bs, ts)

    def ada(w, b, layer=None):
        return _linear(c_all, w, tm=c_rows, tn=1024, out_dtype=F32, silu_in=True, bias=b, layer=layer, name="adaln")

    mod = ada(ada_w, ada_b[0], layer=0)
    sh_m, sc_m, gt_m, sh_f, sc_f, gt_f = [groups(m) for m in jnp.split(mod, 6, axis=-1)]
    w_in = a_w_in[0].astype(BF16)
    n_main = conv_dim + val_dim
    qkvz = _linear(x, w_in, tm=tm_big, tn=1024, out_dtype=F32, norm=True, mod=(sh_m, sc_m), n_cols=n_main,
                   name="gdn_in_proj")
    gate_tn = 128
    assert n_main % gate_tn == 0 and 2 * hv <= gate_tn
    ba = _linear(x, w_in, tm=tm_big, tn=gate_tn, out_dtype=F32, norm=True, mod=(sh_m, sc_m), n_cols=gate_tn,
                 col_blk0=n_main // gate_tn, name="gdn_gate_proj")[:, :2 * hv]
    n_g = hv // 4
    ba_g = ba.reshape(n_tok, 2, n_g, 4).transpose(2, 0, 1, 3).reshape(n_g, n_tok, 8)
    pad4 = jnp.zeros((n_g, 4), F32)
    al_g = jnp.concatenate([pad4, a_a_log[0].reshape(n_g, 4)], axis=1).reshape(n_g, 1, 8)
    dtb_g = jnp.concatenate([pad4, a_dt_bias[0].reshape(n_g, 4)], axis=1).reshape(n_g, 1, 8)
    gdn = functools.partial(_gated_deltanet, qkvz, ba_g, al_g, dtb_g, a_conv_w[0], key_dim=key_dim, val_dim=val_dim)
    o_gdn, delta_p = gdn(jnp.zeros((bp, CONV_W - 1, conv_dim), F32), jnp.zeros((bp, hv, d, d), F32), a_o_norm[0], None,
                         batch=bp, seq=tp, row_off=0, tt_rows=512, chunk=CHUNK, name="gdn_prompt")
    o_gdn, delta_s = gdn(state_conv[0], state_delta[0], a_o_norm[0], o_gdn,
                         batch=bs, seq=ts, row_off=n_p, tt_rows=ts, chunk=min(CHUNK, ts), name="gdn_sample")
    last_rows = lambda r0, t: lax.slice(qkvz, (r0 + t - (CONV_W - 1), 0), (r0 + t, conv_dim))
    conv_p = jnp.stack([last_rows(b * tp, tp) for b in range(bp)])
    conv_s = jnp.stack([last_rows(n_p + b * ts, ts) for b in range(bs)])
    x = _linear(o_gdn, a_w_out[0].astype(BF16), tm=tm_big, tn=512, out_dtype=F32, res=x, gate=gt_m, name="gdn_out_proj")
    x = _moe(x, sh_f, sc_f, gt_f, router_w[0], router_bias[0], w_gate_up, w_down, ws_gate_up, ws_down,
             final_norm, layer=0, final=False, tm=tm_tok)

    mod = ada(ada_w, ada_b[1], layer=1)
    sh_m, sc_m, gt_m, sh_f, sc_f, gt_f = [groups(m) for m in jnp.split(mod, 6, axis=-1)]
    kv_mod = ada(kv_ada_w, kv_ada_b)
    kv_sh, kv_sc = [groups(m) for m in jnp.split(kv_mod, 2, axis=-1)]
    kv = _linear(x, w_kv.astype(BF16), tm=tm_big, tn=1024, out_dtype=F32, norm=True, mod=(kv_sh, kv_sc), name="kv_proj")
    q = _linear(x, b_w_q[0].astype(BF16), tm=tm_big, tn=1024, out_dtype=BF16, norm=True, mod=(sh_m, sc_m), name="q_proj")
    kd = n_kv * d
    k_p = kv[:n_p, :kd].reshape(bp, tp, n_kv, d)
    v_p = kv[:n_p, kd:].reshape(bp, tp, n_kv, d)
    k_s = kv[n_p:, :kd].reshape(bs, ts, n_kv, d)
    v_s = kv[n_p:, kd:].reshape(bs, ts, n_kv, d)
    kv_len_s = -(-(past + ts) // SB_KEY_TILE) * SB_KEY_TILE
    tail = jnp.zeros((bs, kv_len_s - past - ts, kd), F32)
    k_all = jnp.concatenate([cache_k.reshape(bs, past, kd), k_s.reshape(bs, ts, kd), tail], axis=1).reshape(-1, kd)
    v_all = jnp.concatenate([cache_v.reshape(bs, past, kd), v_s.reshape(bs, ts, kd), tail], axis=1).reshape(-1, kd)
    o_att = _stick_breaking(q, kv, kv, None, batch=bp, seq=tp, q_row_off=0, kv_len=tp, kv_row_off=0, k_col0=0,
                            v_col0=n_kv, n_kv=n_kv, grp=grp, q_start=0, tq=256, name="attn_prompt")
    o_att = _stick_breaking(q, k_all, v_all, o_att, batch=bs, seq=ts, q_row_off=n_p, kv_len=kv_len_s, kv_row_off=0,
                            k_col0=0, v_col0=0, n_kv=n_kv, grp=grp, q_start=past, tq=ts, name="attn_sample")
    x = _linear(o_att, b_w_out[0].astype(BF16), tm=tm_big, tn=1024, out_dtype=F32, res=x, gate=gt_m, name="attn_out_proj")
    assert n_s == tm_tok
    y_p, y_s = _moe(x, sh_f, sc_f, gt_f, router_w[1], router_bias[1], w_gate_up, w_down, ws_gate_up, ws_down,
                    final_norm, layer=1, final=True, tm=tm_tok, n_first_rows=n_p)
    y_p = y_p.reshape(bp, tp, dm)
    y_s = y_s.reshape(bs, ts, dm)
    return (y_p, y_s, k_p, v_p, k_s, v_s, delta_p[None], conv_p[None], delta_s[None], conv_s[None])
```
